```python
import jax, jax.numpy as jnp
from jax import lax
import numpy as np

D_MODEL = 1024
BATCH = 16
SEQ = 4096
DEPTH = 1
DEC_BATCH = 16
DEC_SEQ = 16
PAST_LEN = 4096

CHUNK = 64
SGU_CHUNK = 128
SGU_GROUPS = 8
SGU_WIDTH = D_MODEL
SGU_GROUP_DIM = SGU_WIDTH // SGU_GROUPS
N_HEADS = 16
KV_HEADS = 4
HEAD_DIM = 64
Q_GROUP = N_HEADS // KV_HEADS
SWA_WINDOW = 128
ROT_DIM = HEAD_DIM // 4
ROPE_THETA = 500000.0
D_FF = 4 * D_MODEL
NORM_EPS = 1e-6
IN_SPLITS = (SGU_WIDTH, SGU_WIDTH, N_HEADS * HEAD_DIM, KV_HEADS * HEAD_DIM, KV_HEADS * HEAD_DIM, D_MODEL, D_MODEL)
IN_WIDTH = sum(IN_SPLITS)

kernel_name = "streaming_sgu_swa_parallel_gated_encoder_step"


def _rmsnorm(x, g):
    xf = x.astype(jnp.float32)
    y = xf * lax.rsqrt(jnp.mean(xf * xf, axis=-1, keepdims=True) + NORM_EPS)
    return (y * g.astype(jnp.float32)).astype(x.dtype)


def _layernorm(x, g, b):
    xf = x.astype(jnp.float32)
    xc = xf - jnp.mean(xf, axis=-1, keepdims=True)
    y = xc * lax.rsqrt(jnp.mean(xc * xc, axis=-1, keepdims=True) + NORM_EPS)
    return (y * g.astype(jnp.float32) + b.astype(jnp.float32)).astype(x.dtype)


def _rope(x, pos):
    half = ROT_DIM // 2
    inv = ROPE_THETA ** (-jnp.arange(half, dtype=jnp.float32) * 2.0 / ROT_DIM)
    ang = pos[:, None] * inv[None, :]
    cos = jnp.cos(ang)[:, None, :]
    sin = jnp.sin(ang)[:, None, :]
    xr = x[..., :ROT_DIM].astype(jnp.float32)
    x1, x2 = xr[..., :half], xr[..., half:]
    rot = jnp.concatenate([x1 * cos - x2 * sin, x2 * cos + x1 * sin], axis=-1).astype(x.dtype)
    return jnp.concatenate([rot, x[..., ROT_DIM:]], axis=-1)


def _split_in(z):
    offs = np.cumsum(IN_SPLITS)[:-1].tolist()
    return jnp.split(z, offs, axis=-1)


def _sgu_mask():
    i = jnp.arange(SGU_CHUNK)
    return (i[:, None] // CHUNK) >= (i[None, :] // CHUNK)


def _sgu_prompt(u, v, ln_g, ln_b, w_s, b_s):
    B, S = u.shape[:2]
    vn = _layernorm(jax.nn.gelu(v, approximate=False), ln_g, ln_b)
    w = jnp.where(_sgu_mask()[None], w_s, 0)
    vb = vn.reshape(B, S // SGU_CHUNK, SGU_CHUNK, SGU_GROUPS, SGU_GROUP_DIM)
    s = jnp.einsum('gij,bnjgc->bnigc', w, vb) + b_s.T[:, :, None]
    return jax.nn.gelu(u, approximate=False) * s.reshape(B, S, SGU_WIDTH)


def _sgu_sample(u, v, ln_g, ln_b, w_s, b_s):
    B, T = u.shape[:2]
    vn = _layernorm(jax.nn.gelu(v, approximate=False), ln_g, ln_b)
    w = jnp.where(_sgu_mask()[None], w_s, 0)[:, :T, :T]
    vb = vn.reshape(B, T, SGU_GROUPS, SGU_GROUP_DIM)
    s = jnp.einsum('gij,bjgc->bigc', w, vb) + b_s[:, :T].T[:, :, None]
    return jax.nn.gelu(u, approximate=False) * s.reshape(B, T, SGU_WIDTH), vn


def _sink_attention(q, k, v, sinks, mask):
    s = jnp.einsum('...qhgd,...khd->...hgqk', q, k, preferred_element_type=jnp.float32) * (HEAD_DIM ** -0.5)
    if mask is not None:
        s = jnp.where(mask, s, -1e30)
    sink = sinks.astype(jnp.float32).reshape(KV_HEADS, Q_GROUP)[:, :, None, None]
    m = jnp.maximum(jnp.max(s, axis=-1, keepdims=True), sink)
    e = jnp.exp(s - m)
    p = e / (jnp.sum(e, axis=-1, keepdims=True) + jnp.exp(sink - m))
    return jnp.einsum('...hgqk,...khd->...qhgd', p.astype(v.dtype), v)


def _swa_prompt(q, k, v, sinks):
    B, S = q.shape[:2]
    n = S // CHUNK
    nb = SWA_WINDOW // CHUNK
    pos = jnp.arange(S, dtype=jnp.float32)
    q = _rope(q, pos)
    k = _rope(k, pos)
    pad = ((0, 0), (SWA_WINDOW, 0), (0, 0), (0, 0))
    kp = jnp.pad(k, pad).reshape(B, n + nb, CHUNK, KV_HEADS, HEAD_DIM)
    vp = jnp.pad(v, pad).reshape(B, n + nb, CHUNK, KV_HEADS, HEAD_DIM)
    kb = jnp.concatenate([kp[:, i:i + n] for i in range(nb + 1)], axis=2)
    vb = jnp.concatenate([vp[:, i:i + n] for i in range(nb + 1)], axis=2)
    qb = q.reshape(B, n, CHUNK, KV_HEADS, Q_GROUP, HEAD_DIM)
    key_pos = jnp.arange(n)[:, None] * CHUNK - SWA_WINDOW + jnp.arange(SWA_WINDOW + CHUNK)[None, :]
    mask = (key_pos >= 0)[:, None, None, None, :]
    o = _sink_attention(qb, kb, vb, sinks, mask).reshape(B, S, N_HEADS * HEAD_DIM)
    return o, k[:, S - SWA_WINDOW:], v[:, S - SWA_WINDOW:]


def _swa_sample(q, k, v, cache_k, cache_v, sinks):
    B, T = q.shape[:2]
    pos = PAST_LEN + jnp.arange(T, dtype=jnp.float32)
    q = _rope(q, pos)
    k = _rope(k, pos)
    ka = jnp.concatenate([cache_k, k], axis=1)
    va = jnp.concatenate([cache_v, v], axis=1)
    qb = q.reshape(B, T, KV_HEADS, Q_GROUP, HEAD_DIM)
    o = _sink_attention(qb, ka, va, sinks, None).reshape(B, T, N_HEADS * HEAD_DIM)
    return o, k, v


def _merge(a, b, ga, gb, w_branch_a, w_branch_b, w_out):
    m = jax.nn.sigmoid(ga) * (a @ w_branch_a) + jax.nn.sigmoid(gb) * (b @ w_branch_b)
    return m @ w_out


def _ffn(h, g_pre, g_post, w_ff1, w_ff2):
    z = jnp.square(jax.nn.relu(_rmsnorm(h, g_pre) @ w_ff1)) @ w_ff2
    return h + _rmsnorm(z, g_post)


def _layer(h_p, h_s, ck, cv, w_in, sgu_ln_g, sgu_ln_b, sgu_w, sgu_b, attn_sinks, w_branch_a, w_branch_b,
           w_out, g_mix_pre, g_mix_post, g_ffn_pre, g_ffn_post, w_ff1, w_ff2):
    B, S = h_p.shape[:2]
    u, v, q, k, vv, ga, gb = _split_in(_rmsnorm(h_p, g_mix_pre) @ w_in)
    a = _sgu_prompt(u, v, sgu_ln_g, sgu_ln_b, sgu_w, sgu_b)
    b, k_last, v_last = _swa_prompt(q.reshape(B, S, N_HEADS, HEAD_DIM), k.reshape(B, S, KV_HEADS, HEAD_DIM),
                                    vv.reshape(B, S, KV_HEADS, HEAD_DIM), attn_sinks)
    h_p = h_p + _rmsnorm(_merge(a, b, ga, gb, w_branch_a, w_branch_b, w_out), g_mix_post)
    h_p = _ffn(h_p, g_ffn_pre, g_ffn_post, w_ff1, w_ff2)
    Bs, T = h_s.shape[:2]
    u, v, q, k, vv, ga, gb = _split_in(_rmsnorm(h_s, g_mix_pre) @ w_in)
    a, v_sgu = _sgu_sample(u, v, sgu_ln_g, sgu_ln_b, sgu_w, sgu_b)
    b, k_new, v_new = _swa_sample(q.reshape(Bs, T, N_HEADS, HEAD_DIM), k.reshape(Bs, T, KV_HEADS, HEAD_DIM),
                                  vv.reshape(Bs, T, KV_HEADS, HEAD_DIM), ck, cv, attn_sinks)
    h_s = h_s + _rmsnorm(_merge(a, b, ga, gb, w_branch_a, w_branch_b, w_out), g_mix_post)
    h_s = _ffn(h_s, g_ffn_pre, g_ffn_post, w_ff1, w_ff2)
    return h_p, h_s, k_last, v_last, k_new, v_new, v_sgu


def setup_inputs(seed: int = 0) -> dict:
    key = jax.random.key(seed)
    ks = jax.random.split(key, 20)
    f32 = jnp.float32

    def nrm(k, shape, scale):
        return jax.random.normal(k, shape, f32) * scale

    return {
        "x_prompt": nrm(ks[0], (BATCH, SEQ, D_MODEL), 1.0),
        "x_sample": nrm(ks[1], (DEC_BATCH, DEC_SEQ, D_MODEL), 1.0),
        "cache_swa_k": nrm(ks[2], (DEPTH, DEC_BATCH, SWA_WINDOW, KV_HEADS, HEAD_DIM), 1.0),
        "cache_swa_v": nrm(ks[3], (DEPTH, DEC_BATCH, SWA_WINDOW, KV_HEADS, HEAD_DIM), 1.0),
        "w_in": nrm(ks[4], (DEPTH, D_MODEL, IN_WIDTH), D_MODEL ** -0.5),
        "sgu_ln_g": 1.0 + nrm(ks[5], (DEPTH, SGU_WIDTH), 0.02),
        "sgu_ln_b": nrm(ks[6], (DEPTH, SGU_WIDTH), 0.02),
        "sgu_w": nrm(ks[7], (DEPTH, SGU_GROUPS, SGU_CHUNK, SGU_CHUNK), SGU_CHUNK ** -0.5),
        "sgu_b": 1.0 + nrm(ks[8], (DEPTH, SGU_GROUPS, SGU_CHUNK), 0.02),
        "attn_sinks": nrm(ks[9], (DEPTH, N_HEADS), 0.5),
        "w_branch_a": nrm(ks[10], (DEPTH, SGU_WIDTH, D_MODEL), SGU_WIDTH ** -0.5),
        "w_branch_b": nrm(ks[11], (DEPTH, N_HEADS * HEAD_DIM, D_MODEL), (N_HEADS * HEAD_DIM) ** -0.5),
        "w_out": nrm(ks[12], (DEPTH, D_MODEL, D_MODEL), D_MODEL ** -0.5),
        "g_mix_pre": 1.0 + nrm(ks[13], (DEPTH, D_MODEL), 0.02),
        "g_mix_post": 1.0 + nrm(ks[14], (DEPTH, D_MODEL), 0.02),
        "g_ffn_pre": 1.0 + nrm(ks[15], (DEPTH, D_MODEL), 0.02),
        "g_ffn_post": 1.0 + nrm(ks[16], (DEPTH, D_MODEL), 0.02),
        "w_ff1": nrm(ks[17], (DEPTH, D_MODEL, D_FF), D_MODEL ** -0.5),
        "w_ff2": nrm(ks[18], (DEPTH, D_FF, D_MODEL), D_FF ** -0.5),
    }


def reference(x_prompt, x_sample, cache_swa_k, cache_swa_v, w_in, sgu_ln_g, sgu_ln_b, sgu_w, sgu_b, attn_sinks,
              w_branch_a, w_branch_b, w_out, g_mix_pre, g_mix_post, g_ffn_pre, g_ffn_post, w_ff1, w_ff2):
    h_p, h_s = x_prompt, x_sample
    kp_l, vp_l, ks_l, vs_l, us_l = [], [], [], [], []
    for l in range(DEPTH):
        h_p, h_s, kp, vp, ksn, vsn, usn = _layer(
            h_p, h_s, cache_swa_k[l], cache_swa_v[l], w_in[l], sgu_ln_g[l], sgu_ln_b[l], sgu_w[l], sgu_b[l],
            attn_sinks[l], w_branch_a[l], w_branch_b[l], w_out[l], g_mix_pre[l], g_mix_post[l], g_ffn_pre[l],
            g_ffn_post[l], w_ff1[l], w_ff2[l])
        kp_l.append(kp)
        vp_l.append(vp)
        ks_l.append(ksn)
        vs_l.append(vsn)
        us_l.append(usn)
    return (h_p, h_s, jnp.stack(kp_l), jnp.stack(vp_l), jnp.stack(ks_l), jnp.stack(vs_l), jnp.stack(us_l))
```

```python
import functools

import numpy as np
import jax
import jax.numpy as jnp
from jax import lax
from jax.experimental import pallas as pl
from jax.experimental.pallas import tpu as pltpu

D_MODEL = 1024
CHUNK = 64
SGU_CHUNK = 128
SGU_GROUPS = 8
SGU_GROUP_DIM = D_MODEL // SGU_GROUPS
N_HEADS = 16
KV_HEADS = 4
HEAD_DIM = 64
Q_GROUP = N_HEADS // KV_HEADS
SWA_WINDOW = 128
ROT_DIM = HEAD_DIM // 4
ROPE_THETA = 500000.0
D_FF = 4 * D_MODEL
NORM_EPS = 1e-6
PAST_LEN = 4096
MASK_VALUE = -1e30

Q_WIDTH = N_HEADS * HEAD_DIM
KV_WIDTH = KV_HEADS * HEAD_DIM
GROUP_WIDTH = Q_GROUP * HEAD_DIM
OFF_U = 0
OFF_V = OFF_U + D_MODEL
OFF_Q = OFF_V + D_MODEL
OFF_K = OFF_Q + Q_WIDTH
OFF_VA = OFF_K + KV_WIDTH
OFF_GA = OFF_VA + KV_WIDTH
OFF_GB = OFF_GA + D_MODEL
IN_WIDTH = OFF_GB + D_MODEL

LANES = 128
MXU_COLS = 256
VMEM_LIMIT_BYTES = 52 * 1024 * 1024
MIX_TILE = 512
FFN_TILE = 512
ATT_BLOCK = 2 * CHUNK
ATT_KEYS = ATT_BLOCK + SWA_WINDOW
FF_CHUNK = 1024

F32 = jnp.float32
BF16 = jnp.bfloat16
SQRT_HALF = np.sqrt(0.5).astype(np.float32)


def _dot(a, b):
    return jnp.dot(a, b, preferred_element_type=F32)


def _dot_nt(a, b):
    return lax.dot_general(a, b, (((1,), (1,)), ((), ())), preferred_element_type=F32)


def _rmsnorm(x, g):
    return x * lax.rsqrt(jnp.mean(x * x, axis=-1, keepdims=True) + NORM_EPS) * g


def _layernorm(x, g, b):
    xc = x - jnp.mean(x, axis=-1, keepdims=True)
    return xc * lax.rsqrt(jnp.mean(xc * xc, axis=-1, keepdims=True) + NORM_EPS) * g + b


def _gelu(x):
    return 0.5 * x * (1.0 + lax.erf(x * SQRT_HALF))


def _rope(x, cos, sin_up, sin_dn):
    width = x.shape[1]
    up = pltpu.roll(x, ROT_DIM // 2, axis=1)
    dn = pltpu.roll(x, width - ROT_DIM // 2, axis=1)
    reps = width // LANES
    tile = lambda t: jnp.concatenate([t] * reps, axis=1) if reps > 1 else t
    return x * tile(cos) + up * tile(sin_up) + dn * tile(sin_dn)


def _group_lane_masks(rows):
    lane = lax.broadcasted_iota(jnp.int32, (rows, GROUP_WIDTH), 1)
    return [(lane >= g * HEAD_DIM) & (lane < (g + 1) * HEAD_DIM) for g in range(Q_GROUP)]


def _masked_sgu_weights(sguw_ref, size):
    i = lax.broadcasted_iota(jnp.int32, (size, size), 0) // CHUNK
    j = lax.broadcasted_iota(jnp.int32, (size, size), 1) // CHUNK
    keep = i >= j
    return [jnp.where(keep, sguw_ref[g, :size, :size], 0.0).astype(BF16) for g in range(SGU_GROUPS)]


def _sink_softmax_attention(q_bf, k_win, v_win, mask, sinks, rows, lane_masks, lane_masks_bf):
    q_stack = jnp.concatenate([q_bf * lane_masks_bf[g] for g in range(Q_GROUP)], axis=0)
    s = _dot_nt(q_stack, k_win)
    es, invs = [], []
    for g in range(Q_GROUP):
        sg = s[g * rows:(g + 1) * rows]
        if mask is not None:
            sg = jnp.where(mask, sg, MASK_VALUE)
        m = jnp.maximum(jnp.max(sg, axis=-1, keepdims=True), sinks[g])
        e = jnp.exp(sg - m)
        den = jnp.sum(e, axis=-1, keepdims=True) + jnp.exp(sinks[g] - m)
        es.append(e.astype(BF16))
        invs.append(1.0 / den)
    o = _dot(jnp.concatenate(es, axis=0), v_win)
    out = None
    for g in range(Q_GROUP):
        og = jnp.where(lane_masks[g], o[g * rows:(g + 1) * rows] * invs[g], 0.0)
        out = og if out is None else out + og
    return out


def _merge_and_residual(x, xn_ref, a_ref, b_ref, win_ref, wa_ref, wb_ref, wo_ref, gpost):
    ga = jax.nn.sigmoid(_dot(xn_ref[...], win_ref[:, OFF_GA:OFF_GA + D_MODEL]))
    m = ga * _dot(a_ref[...], wa_ref[...])
    gb = jax.nn.sigmoid(_dot(xn_ref[...], win_ref[:, OFF_GB:OFF_GB + D_MODEL]))
    m = m + gb * _dot(b_ref[...], wb_ref[...])
    out = _dot(m.astype(BF16), wo_ref[...])
    return x + _rmsnorm(out, gpost)


def _mixer_prompt_kernel(sinks_ref, x_ref, cos_ref, sup_ref, sdn_ref, gpre_ref, gpost_ref, win_ref,
                         lng_ref, lnb_ref, sguw_ref, sgub_ref, rep_ref, wa_ref, wb_ref, wo_ref,
                         h_ref, klast_ref, vlast_ref,
                         xn_ref, gv_ref, vn_ref, a_ref, q_ref, kext_ref, vext_ref, b_ref):
    T = MIX_TILE
    t = pl.program_id(1)

    @pl.when(t == 0)
    def _():
        kext_ref[0:SWA_WINDOW] = jnp.zeros((SWA_WINDOW, Q_WIDTH), BF16)
        vext_ref[0:SWA_WINDOW] = jnp.zeros((SWA_WINDOW, Q_WIDTH), BF16)

    @pl.when(t > 0)
    def _():
        kext_ref[0:SWA_WINDOW] = kext_ref[T:T + SWA_WINDOW]
        vext_ref[0:SWA_WINDOW] = vext_ref[T:T + SWA_WINDOW]

    x = x_ref[...]
    xn_ref[...] = _rmsnorm(x, gpre_ref[...]).astype(BF16)

    for c in range(D_MODEL // MXU_COLS):
        cols = slice(c * MXU_COLS, (c + 1) * MXU_COLS)
        gv_ref[:, cols] = _gelu(_dot(xn_ref[...], win_ref[:, OFF_V + c * MXU_COLS:OFF_V + (c + 1) * MXU_COLS]))
    vn_ref[...] = _layernorm(gv_ref[...], lng_ref[...], lnb_ref[...]).astype(BF16)

    w_sgu = _masked_sgu_weights(sguw_ref, SGU_CHUNK)
    n_chunks = T // SGU_CHUNK
    groups_per_step = MXU_COLS // SGU_GROUP_DIM
    for c in range(D_MODEL // MXU_COLS):
        gu = _gelu(_dot(xn_ref[...], win_ref[:, OFF_U + c * MXU_COLS:OFF_U + (c + 1) * MXU_COLS]))
        for gg in range(groups_per_step):
            g = c * groups_per_step + gg
            gcols = slice(g * SGU_GROUP_DIM, (g + 1) * SGU_GROUP_DIM)
            rhs = jnp.concatenate(
                [vn_ref[n * SGU_CHUNK:(n + 1) * SGU_CHUNK, gcols] for n in range(n_chunks)], axis=1)
            s = _dot(w_sgu[g], rhs)
            bias = sgub_ref[:, gcols]
            for n in range(n_chunks):
                rows = slice(n * SGU_CHUNK, (n + 1) * SGU_CHUNK)
                sn = s[:, n * SGU_GROUP_DIM:(n + 1) * SGU_GROUP_DIM] + bias
                a_ref[rows, gcols] = (gu[rows, gg * SGU_GROUP_DIM:(gg + 1) * SGU_GROUP_DIM] * sn).astype(BF16)

    cos, sup, sdn = cos_ref[...], sup_ref[...], sdn_ref[...]
    q = _rope(_dot(xn_ref[...], win_ref[:, OFF_Q:OFF_Q + Q_WIDTH]), cos, sup, sdn)
    q_ref[...] = (q * (HEAD_DIM ** -0.5)).astype(BF16)
    k = _rope(_dot(xn_ref[...], win_ref[:, OFF_K:OFF_K + KV_WIDTH]), cos, sup, sdn)
    va = _dot(xn_ref[...], win_ref[:, OFF_VA:OFF_VA + KV_WIDTH])
    klast_ref[...] = k[T - SWA_WINDOW:]
    vlast_ref[...] = va[T - SWA_WINDOW:]
    kext_ref[SWA_WINDOW:] = _dot(k.astype(BF16), rep_ref[...]).astype(BF16)
    vext_ref[SWA_WINDOW:] = _dot(va.astype(BF16), rep_ref[...]).astype(BF16)

    lane_masks = _group_lane_masks(ATT_BLOCK)
    lane_masks_bf = [m.astype(F32).astype(BF16) for m in lane_masks]
    q_chunk = lax.broadcasted_iota(jnp.int32, (ATT_BLOCK, ATT_KEYS), 0) // CHUNK
    key_idx = lax.broadcasted_iota(jnp.int32, (ATT_BLOCK, ATT_KEYS), 1)
    key_chunk = key_idx // CHUNK
    band = (key_chunk >= q_chunk) & (key_chunk <= q_chunk + SWA_WINDOW // CHUNK)
    has_prev = jnp.where(t > 0, SWA_WINDOW, 0)
    band_first = band & (key_idx + has_prev >= SWA_WINDOW)
    for i in range(T // ATT_BLOCK):
        rows = slice(i * ATT_BLOCK, (i + 1) * ATT_BLOCK)
        krows = slice(i * ATT_BLOCK, i * ATT_BLOCK + ATT_KEYS)
        mask = band_first if i == 0 else band
        for h in range(KV_HEADS):
            hcols = slice(h * GROUP_WIDTH, (h + 1) * GROUP_WIDTH)
            sinks = [sinks_ref[h * Q_GROUP + g] for g in range(Q_GROUP)]
            o = _sink_softmax_attention(q_ref[rows, hcols], kext_ref[krows, hcols], vext_ref[krows, hcols],
                                        mask, sinks, ATT_BLOCK, lane_masks, lane_masks_bf)
            b_ref[rows, hcols] = o.astype(BF16)

    h_ref[...] = _merge_and_residual(x, xn_ref, a_ref, b_ref, win_ref, wa_ref, wb_ref, wo_ref, gpost_ref[...])


def _mixer_sample_kernel(sinks_ref, x_ref, cos_ref, sup_ref, sdn_ref, gpre_ref, gpost_ref, win_ref,
                         lng_ref, lnb_ref, sguw_ref, sgub_ref, rep_ref, wa_ref, wb_ref, wo_ref,
                         ck_ref, cv_ref,
                         h_ref, knew_ref, vnew_ref, vsgu_ref,
                         xn_ref, vn_ref, a_ref, q_ref, ckrep_ref, cvrep_ref, knrep_ref, vnrep_ref, b_ref,
                         *, n_seq, seq_len):
    x = x_ref[...]
    xn_ref[...] = _rmsnorm(x, gpre_ref[...]).astype(BF16)

    vn = _layernorm(_gelu(_dot(xn_ref[...], win_ref[:, OFF_V:OFF_V + D_MODEL])), lng_ref[...], lnb_ref[...])
    vsgu_ref[...] = vn
    vn_ref[...] = vn.astype(BF16)
    gu = _gelu(_dot(xn_ref[...], win_ref[:, OFF_U:OFF_U + D_MODEL]))
    w_sgu = _masked_sgu_weights(sguw_ref, seq_len)
    for g in range(SGU_GROUPS):
        gcols = slice(g * SGU_GROUP_DIM, (g + 1) * SGU_GROUP_DIM)
        rhs = jnp.concatenate([vn_ref[n * seq_len:(n + 1) * seq_len, gcols] for n in range(n_seq)], axis=1)
        s = _dot(w_sgu[g], rhs)
        bias = sgub_ref[0:seq_len, gcols]
        for n in range(n_seq):
            rows = slice(n * seq_len, (n + 1) * seq_len)
            sn = s[:, n * SGU_GROUP_DIM:(n + 1) * SGU_GROUP_DIM] + bias
            a_ref[rows, gcols] = (gu[rows, gcols] * sn).astype(BF16)

    cos, sup, sdn = cos_ref[...], sup_ref[...], sdn_ref[...]
    q = _rope(_dot(xn_ref[...], win_ref[:, OFF_Q:OFF_Q + Q_WIDTH]), cos, sup, sdn)
    q_ref[...] = (q * (HEAD_DIM ** -0.5)).astype(BF16)
    k = _rope(_dot(xn_ref[...], win_ref[:, OFF_K:OFF_K + KV_WIDTH]), cos, sup, sdn)
    va = _dot(xn_ref[...], win_ref[:, OFF_VA:OFF_VA + KV_WIDTH])
    knew_ref[...] = k
    vnew_ref[...] = va
    knrep_ref[...] = _dot(k.astype(BF16), rep_ref[...]).astype(BF16)
    vnrep_ref[...] = _dot(va.astype(BF16), rep_ref[...]).astype(BF16)
    ckrep_ref[...] = _dot(ck_ref[...].astype(BF16), rep_ref[...]).astype(BF16)
    cvrep_ref[...] = _dot(cv_ref[...].astype(BF16), rep_ref[...]).astype(BF16)

    n_keys = SWA_WINDOW + seq_len
    pad = jnp.zeros((ATT_KEYS - n_keys, GROUP_WIDTH), BF16)
    lane_masks = _group_lane_masks(seq_len)
    lane_masks_bf = [m.astype(F32).astype(BF16) for m in lane_masks]
    mask = lax.broadcasted_iota(jnp.int32, (seq_len, ATT_KEYS), 1) < n_keys

    def per_sequence(n, carry):
        new_rows = pl.ds(pl.multiple_of(n * seq_len, seq_len), seq_len)
        old_rows = pl.ds(pl.multiple_of(n * SWA_WINDOW, SWA_WINDOW), SWA_WINDOW)
        for h in range(KV_HEADS):
            hcols = slice(h * GROUP_WIDTH, (h + 1) * GROUP_WIDTH)
            sinks = [sinks_ref[h * Q_GROUP + g] for g in range(Q_GROUP)]
            k_win = jnp.concatenate([ckrep_ref[old_rows, hcols], knrep_ref[new_rows, hcols], pad], axis=0)
            v_win = jnp.concatenate([cvrep_ref[old_rows, hcols], vnrep_ref[new_rows, hcols], pad], axis=0)
            o = _sink_softmax_attention(q_ref[new_rows, hcols], k_win, v_win, mask, sinks, seq_len,
                                        lane_masks, lane_masks_bf)
            b_ref[new_rows, hcols] = o.astype(BF16)
        return carry

    lax.fori_loop(0, n_seq, per_sequence, 0)

    h_ref[...] = _merge_and_residual(x, xn_ref, a_ref, b_ref, win_ref, wa_ref, wb_ref, wo_ref, gpost_ref[...])


def _ffn_kernel(h_ref, gpre_ref, gpost_ref, w1_ref, w2_ref, y_ref, hn_ref):
    h = h_ref[...]
    hn_ref[...] = _rmsnorm(h, gpre_ref[...]).astype(BF16)
    z = None
    for j in range(D_FF // FF_CHUNK):
        f = _dot(hn_ref[...], w1_ref[:, j * FF_CHUNK:(j + 1) * FF_CHUNK])
        f = jnp.square(jnp.maximum(f, 0.0)).astype(BF16)
        zj = _dot(f, w2_ref[j * FF_CHUNK:(j + 1) * FF_CHUNK, :])
        z = zj if z is None else z + zj
    y_ref[...] = h + _rmsnorm(z, gpost_ref[...])


def _resident(shape):
    return pl.BlockSpec(shape, lambda *_: (0,) * len(shape), pipeline_mode=pl.Buffered(1))


def _rope_tables(pos):
    half = ROT_DIM // 2
    inv = ROPE_THETA ** (-jnp.arange(half, dtype=F32) * 2.0 / ROT_DIM)
    ang = pos[:, None] * inv[None, :]
    cos, sin = jnp.cos(ang), jnp.sin(ang)
    n = pos.shape[0]
    zeros = lambda w: jnp.zeros((n, w), F32)
    cos_t = jnp.concatenate([cos, cos, jnp.ones((n, HEAD_DIM - ROT_DIM), F32)], axis=1)
    sup_t = jnp.concatenate([zeros(half), sin, zeros(HEAD_DIM - ROT_DIM)], axis=1)
    sdn_t = jnp.concatenate([-sin, zeros(HEAD_DIM - half)], axis=1)
    tile = lambda t: jnp.tile(t, (1, LANES // HEAD_DIM))
    return tile(cos_t), tile(sup_t), tile(sdn_t)


def _kv_replication_matrix():
    src = np.arange(KV_WIDTH)
    dst = np.arange(Q_WIDTH)
    same_head = (dst[None, :] // GROUP_WIDTH) == (src[:, None] // HEAD_DIM)
    same_dim = (dst[None, :] % HEAD_DIM) == (src[:, None] % HEAD_DIM)
    return jnp.asarray((same_head & same_dim).astype(np.float32), dtype=BF16)


def _shared_mixer_operands(w):
    specs = [
        _resident((1, D_MODEL)), _resident((1, D_MODEL)), _resident((D_MODEL, IN_WIDTH)),
        _resident((1, D_MODEL)), _resident((1, D_MODEL)),
        _resident((SGU_GROUPS, SGU_CHUNK, SGU_CHUNK)), _resident((SGU_CHUNK, D_MODEL)),
        _resident((KV_WIDTH, Q_WIDTH)),
        _resident((D_MODEL, D_MODEL)), _resident((Q_WIDTH, D_MODEL)), _resident((D_MODEL, D_MODEL)),
    ]
    args = [w["g_mix_pre"], w["g_mix_post"], w["w_in"], w["sgu_ln_g"], w["sgu_ln_b"], w["sgu_w"],
            w["sgu_b_rows"], w["rep"], w["w_branch_a"], w["w_branch_b"], w["w_out"]]
    return specs, args


def _mixer_prompt(x, w):
    B, S, _ = x.shape
    T = MIX_TILE
    cos, sup, sdn = _rope_tables(jnp.arange(S, dtype=F32))
    table_spec = pl.BlockSpec((T, LANES), lambda b, t: (t, 0))
    shared_specs, shared_args = _shared_mixer_operands(w)
    last_spec = pl.BlockSpec((None, SWA_WINDOW, KV_WIDTH), lambda b, t: (b, 0, 0))
    return pl.pallas_call(
        _mixer_prompt_kernel,
        grid=(B, S // T),
        in_specs=[pl.BlockSpec(memory_space=pltpu.SMEM),
                  pl.BlockSpec((None, T, D_MODEL), lambda b, t: (b, t, 0)),
                  table_spec, table_spec, table_spec] + shared_specs,
        out_specs=[pl.BlockSpec((None, T, D_MODEL), lambda b, t: (b, t, 0)), last_spec, last_spec],
        out_shape=[jax.ShapeDtypeStruct((B, S, D_MODEL), F32),
                   jax.ShapeDtypeStruct((B, SWA_WINDOW, KV_WIDTH), F32),
                   jax.ShapeDtypeStruct((B, SWA_WINDOW, KV_WIDTH), F32)],
        scratch_shapes=[
            pltpu.VMEM((T, D_MODEL), BF16),
            pltpu.VMEM((T, D_MODEL), F32),
            pltpu.VMEM((T, D_MODEL), BF16),
            pltpu.VMEM((T, D_MODEL), BF16),
            pltpu.VMEM((T, Q_WIDTH), BF16),
            pltpu.VMEM((T + SWA_WINDOW, Q_WIDTH), BF16),
            pltpu.VMEM((T + SWA_WINDOW, Q_WIDTH), BF16),
            pltpu.VMEM((T, Q_WIDTH), BF16),
        ],
        compiler_params=pltpu.CompilerParams(
            dimension_semantics=("arbitrary", "arbitrary"), vmem_limit_bytes=VMEM_LIMIT_BYTES),
        name="mixer_prompt",
    )(w["attn_sinks"], x, cos, sup, sdn, *shared_args)


def _mixer_sample(x, cache_k, cache_v, w):
    n_seq, seq_len, _ = x.shape
    rows = n_seq * seq_len
    cos, sup, sdn = _rope_tables(PAST_LEN + jnp.arange(seq_len, dtype=F32))
    tables = [jnp.tile(t, (n_seq, 1)) for t in (cos, sup, sdn)]
    shared_specs, shared_args = _shared_mixer_operands(w)
    cache_rows = n_seq * SWA_WINDOW
    return pl.pallas_call(
        functools.partial(_mixer_sample_kernel, n_seq=n_seq, seq_len=seq_len),
        grid=(1,),
        in_specs=[pl.BlockSpec(memory_space=pltpu.SMEM), _resident((rows, D_MODEL)),
                  _resident((rows, LANES)), _resident((rows, LANES)), _resident((rows, LANES))]
                 + shared_specs + [_resident((cache_rows, KV_WIDTH)), _resident((cache_rows, KV_WIDTH))],
        out_specs=[_resident((rows, D_MODEL)), _resident((rows, KV_WIDTH)), _resident((rows, KV_WIDTH)),
                   _resident((rows, D_MODEL))],
        out_shape=[jax.ShapeDtypeStruct((rows, D_MODEL), F32),
                   jax.ShapeDtypeStruct((rows, KV_WIDTH), F32),
                   jax.ShapeDtypeStruct((rows, KV_WIDTH), F32),
                   jax.ShapeDtypeStruct((rows, D_MODEL), F32)],
        scratch_shapes=[
            pltpu.VMEM((rows, D_MODEL), BF16),
            pltpu.VMEM((rows, D_MODEL), BF16),
            pltpu.VMEM((rows, D_MODEL), BF16),
            pltpu.VMEM((rows, Q_WIDTH), BF16),
            pltpu.VMEM((cache_rows, Q_WIDTH), BF16),
            pltpu.VMEM((cache_rows, Q_WIDTH), BF16),
            pltpu.VMEM((rows, Q_WIDTH), BF16),
            pltpu.VMEM((rows, Q_WIDTH), BF16),
            pltpu.VMEM((rows, Q_WIDTH), BF16),
        ],
        compiler_params=pltpu.CompilerParams(
            dimension_semantics=("arbitrary",), vmem_limit_bytes=VMEM_LIMIT_BYTES),
        name="mixer_sample",
    )(w["attn_sinks"], x.reshape(rows, D_MODEL), *tables, *shared_args,
      cache_k.reshape(cache_rows, KV_WIDTH), cache_v.reshape(cache_rows, KV_WIDTH))


def _ffn(h, w, tile, name):
    rows = h.shape[0]
    return pl.pallas_call(
        _ffn_kernel,
        grid=(rows // tile,),
        in_specs=[pl.BlockSpec((tile, D_MODEL), lambda i: (i, 0)),
                  _resident((1, D_MODEL)), _resident((1, D_MODEL)),
                  _resident((D_MODEL, D_FF)), _resident((D_FF, D_MODEL))],
        out_specs=pl.BlockSpec((tile, D_MODEL), lambda i: (i, 0)),
        out_shape=jax.ShapeDtypeStruct((rows, D_MODEL), F32),
        scratch_shapes=[pltpu.VMEM((tile, D_MODEL), BF16)],
        compiler_params=pltpu.CompilerParams(
            dimension_semantics=("arbitrary",), vmem_limit_bytes=VMEM_LIMIT_BYTES),
        name=name,
    )(h, w["g_ffn_pre"], w["g_ffn_post"], w["w_ff1"], w["w_ff2"])


def _layer(h_p, h_s, ck, cv, w):
    B, S, _ = h_p.shape
    n_seq, seq_len, _ = h_s.shape
    h_p, k_last, v_last = _mixer_prompt(h_p, w)
    y_p = _ffn(h_p.reshape(B * S, D_MODEL), w, FFN_TILE, "ffn_prompt").reshape(B, S, D_MODEL)
    h_s, k_new, v_new, v_sgu = _mixer_sample(h_s, ck, cv, w)
    y_s = _ffn(h_s, w, n_seq * seq_len, "ffn_sample").reshape(n_seq, seq_len, D_MODEL)
    return (y_p, y_s,
            k_last.reshape(B, SWA_WINDOW, KV_HEADS, HEAD_DIM), v_last.reshape(B, SWA_WINDOW, KV_HEADS, HEAD_DIM),
            k_new.reshape(n_seq, seq_len, KV_HEADS, HEAD_DIM), v_new.reshape(n_seq, seq_len, KV_HEADS, HEAD_DIM),
            v_sgu.reshape(n_seq, seq_len, D_MODEL))


def kernel(x_prompt, x_sample, cache_swa_k, cache_swa_v, w_in, sgu_ln_g, sgu_ln_b, sgu_w, sgu_b, attn_sinks,
           w_branch_a, w_branch_b, w_out, g_mix_pre, g_mix_post, g_ffn_pre, g_ffn_post, w_ff1, w_ff2):
    depth = w_in.shape[0]
    rep = _kv_replication_matrix()
    h_p, h_s = x_prompt, x_sample
    per_layer = []
    for l in range(depth):
        row = lambda v: v[l].reshape(1, -1)
        w = {
            "w_in": w_in[l].astype(BF16), "w_branch_a": w_branch_a[l].astype(BF16),
            "w_branch_b": w_branch_b[l].astype(BF16), "w_out": w_out[l].astype(BF16),
            "w_ff1": w_ff1[l].astype(BF16), "w_ff2": w_ff2[l].astype(BF16),
            "sgu_ln_g": row(sgu_ln_g), "sgu_ln_b": row(sgu_ln_b), "sgu_w": sgu_w[l],
            "sgu_b_rows": jnp.repeat(sgu_b[l].T, SGU_GROUP_DIM, axis=1),
            "attn_sinks": attn_sinks[l], "rep": rep,
            "g_mix_pre": row(g_mix_pre), "g_mix_post": row(g_mix_post),
            "g_ffn_pre": row(g_ffn_pre), "g_ffn_post": row(g_ffn_post),
        }
        h_p, h_s, *states = _layer(h_p, h_s, cache_swa_k[l], cache_swa_v[l], w)
        per_layer.append(states)
    stacked = [jnp.stack([states[i] for states in per_layer]) for i in range(5)]
    return (h_p, h_s, *stacked)
```

```python
import functools

import numpy as np
import jax
import jax.numpy as jnp
from jax import lax
from jax.experimental import pallas as pl
from jax.experimental.pallas import tpu as pltpu

D_MODEL = 1024
CHUNK = 64
SGU_CHUNK = 128
SGU_GROUPS = 8
SGU_GROUP_DIM = D_MODEL // SGU_GROUPS
N_HEADS = 16
KV_HEADS = 4
HEAD_DIM = 64
Q_GROUP = N_HEADS // KV_HEADS
SWA_WINDOW = 128
ROT_DIM = HEAD_DIM // 4
ROPE_THETA = 500000.0
D_FF = 4 * D_MODEL
NORM_EPS = 1e-6
PAST_LEN = 4096
MASK_VALUE = -1e30

Q_WIDTH = N_HEADS * HEAD_DIM
KV_WIDTH = KV_HEADS * HEAD_DIM
OFF_U = 0
OFF_V = OFF_U + D_MODEL
OFF_Q = OFF_V + D_MODEL
OFF_K = OFF_Q + Q_WIDTH
OFF_VA = OFF_K + KV_WIDTH
OFF_GA = OFF_VA + KV_WIDTH
OFF_GB = OFF_GA + D_MODEL
IN_WIDTH = OFF_GB + D_MODEL

LANES = 128
MXU_COLS = 256
VMEM_LIMIT_BYTES = 52 * 1024 * 1024
MIX_TILE = 512
FFN_TILE = 1024
FFN_SUB = 512
ATT_BLOCK = 2 * CHUNK
ATT_KEYS = ATT_BLOCK + SWA_WINDOW
FF_CHUNK = 1024

F32 = jnp.float32
BF16 = jnp.bfloat16
SQRT_HALF = np.sqrt(0.5).astype(np.float32)


def _dot(a, b):
    return jnp.dot(a, b, preferred_element_type=F32)


def _dot_nt(a, b):
    return lax.dot_general(a, b, (((1,), (1,)), ((), ())), preferred_element_type=F32)


def _rmsnorm(x, g):
    return x * lax.rsqrt(jnp.mean(x * x, axis=-1, keepdims=True) + NORM_EPS) * g


def _layernorm(x, g, b):
    xc = x - jnp.mean(x, axis=-1, keepdims=True)
    return xc * lax.rsqrt(jnp.mean(xc * xc, axis=-1, keepdims=True) + NORM_EPS) * g + b


def _gelu(x):
    return 0.5 * x * (1.0 + lax.erf(x * SQRT_HALF))


def _rope(x, cos, sin_up, sin_dn):
    width = x.shape[1]
    up = pltpu.roll(x, ROT_DIM // 2, axis=1)
    dn = pltpu.roll(x, width - ROT_DIM // 2, axis=1)
    reps = width // LANES
    tile = lambda t: jnp.concatenate([t] * reps, axis=1) if reps > 1 else t
    return x * tile(cos) + up * tile(sin_up) + dn * tile(sin_dn)


def _head_lane_masks(rows):
    lane = lax.broadcasted_iota(jnp.int32, (rows, KV_WIDTH), 1)
    return [(lane >= h * HEAD_DIM) & (lane < (h + 1) * HEAD_DIM) for h in range(KV_HEADS)]


def _masked_sgu_weights(sguw_ref, size):
    i = lax.broadcasted_iota(jnp.int32, (size, size), 0) // CHUNK
    j = lax.broadcasted_iota(jnp.int32, (size, size), 1) // CHUNK
    keep = i >= j
    return [jnp.where(keep, sguw_ref[g, :size, :size], 0.0).astype(BF16) for g in range(SGU_GROUPS)]


def _sink_softmax_attention(q_bf, k_win, v_win, mask, sinks, rows, lane_masks, lane_masks_bf):
    q_stack = jnp.concatenate([q_bf * lane_masks_bf[h] for h in range(KV_HEADS)], axis=0)
    s = _dot_nt(q_stack, k_win)
    es, invs = [], []
    for h in range(KV_HEADS):
        sh = jnp.where(mask, s[h * rows:(h + 1) * rows], MASK_VALUE)
        m = jnp.maximum(jnp.max(sh, axis=-1, keepdims=True), sinks[h])
        e = jnp.exp(sh - m)
        den = jnp.sum(e, axis=-1, keepdims=True) + jnp.exp(sinks[h] - m)
        es.append(e.astype(BF16))
        invs.append(1.0 / den)
    o = _dot(jnp.concatenate(es, axis=0), v_win)
    out = o[0:rows] * invs[0]
    for h in range(1, KV_HEADS):
        out = jnp.where(lane_masks[h], o[h * rows:(h + 1) * rows] * invs[h], out)
    return out


def _merge_and_residual(x, xn_ref, a_ref, b_ref, win_ref, wa_ref, wb_ref, wo_ref, gpost):
    ga = jax.nn.sigmoid(_dot(xn_ref[...], win_ref[:, OFF_GA:OFF_GA + D_MODEL]))
    m = ga * _dot(a_ref[...], wa_ref[...])
    gb = jax.nn.sigmoid(_dot(xn_ref[...], win_ref[:, OFF_GB:OFF_GB + D_MODEL]))
    m = m + gb * _dot(b_ref[...], wb_ref[...])
    out = _dot(m.astype(BF16), wo_ref[...])
    return x + _rmsnorm(out, gpost)


def _mixer_prompt_kernel(sinks_ref, x_ref, cos_ref, sup_ref, sdn_ref, gpre_ref, gpost_ref, win_ref,
                         lng_ref, lnb_ref, sguw_ref, sgub_ref, wa_ref, wb_ref, wo_ref,
                         h_ref, klast_ref, vlast_ref,
                         xn_ref, gv_ref, vn_ref, a_ref, q_ref, kext_ref, vext_ref, b_ref):
    T = x_ref.shape[0]
    t = pl.program_id(1)

    @pl.when(t == 0)
    def _():
        kext_ref[0:SWA_WINDOW] = jnp.zeros((SWA_WINDOW, KV_WIDTH), BF16)
        vext_ref[0:SWA_WINDOW] = jnp.zeros((SWA_WINDOW, KV_WIDTH), BF16)

    @pl.when(t > 0)
    def _():
        kext_ref[0:SWA_WINDOW] = kext_ref[T:T + SWA_WINDOW]
        vext_ref[0:SWA_WINDOW] = vext_ref[T:T + SWA_WINDOW]

    x = x_ref[...]
    xn_ref[...] = _rmsnorm(x, gpre_ref[...]).astype(BF16)

    for c in range(D_MODEL // MXU_COLS):
        cols = slice(c * MXU_COLS, (c + 1) * MXU_COLS)
        gv_ref[:, cols] = _gelu(_dot(xn_ref[...], win_ref[:, OFF_V + c * MXU_COLS:OFF_V + (c + 1) * MXU_COLS]))
    vn_ref[...] = _layernorm(gv_ref[...], lng_ref[...], lnb_ref[...]).astype(BF16)

    w_sgu = _masked_sgu_weights(sguw_ref, SGU_CHUNK)
    n_chunks = T // SGU_CHUNK
    groups_per_step = MXU_COLS // SGU_GROUP_DIM
    for c in range(D_MODEL // MXU_COLS):
        gu = _gelu(_dot(xn_ref[...], win_ref[:, OFF_U + c * MXU_COLS:OFF_U + (c + 1) * MXU_COLS]))
        for gg in range(groups_per_step):
            g = c * groups_per_step + gg
            gcols = slice(g * SGU_GROUP_DIM, (g + 1) * SGU_GROUP_DIM)
            rhs = jnp.concatenate(
                [vn_ref[n * SGU_CHUNK:(n + 1) * SGU_CHUNK, gcols] for n in range(n_chunks)], axis=1)
            s = _dot(w_sgu[g], rhs)
            bias = sgub_ref[:, gcols]
            for n in range(n_chunks):
                rows = slice(n * SGU_CHUNK, (n + 1) * SGU_CHUNK)
                sn = s[:, n * SGU_GROUP_DIM:(n + 1) * SGU_GROUP_DIM] + bias
                a_ref[rows, gcols] = (gu[rows, gg * SGU_GROUP_DIM:(gg + 1) * SGU_GROUP_DIM] * sn).astype(BF16)

    cos, sup, sdn = cos_ref[...], sup_ref[...], sdn_ref[...]
    q = _rope(_dot(xn_ref[...], win_ref[:, OFF_Q:OFF_Q + Q_WIDTH]), cos, sup, sdn)
    q_ref[...] = (q * (HEAD_DIM ** -0.5)).astype(BF16)
    k = _rope(_dot(xn_ref[...], win_ref[:, OFF_K:OFF_K + KV_WIDTH]), cos, sup, sdn)
    va = _dot(xn_ref[...], win_ref[:, OFF_VA:OFF_VA + KV_WIDTH])
    klast_ref[...] = k[T - SWA_WINDOW:]
    vlast_ref[...] = va[T - SWA_WINDOW:]
    kext_ref[SWA_WINDOW:] = k.astype(BF16)
    vext_ref[SWA_WINDOW:] = va.astype(BF16)

    lane_masks = _head_lane_masks(ATT_BLOCK)
    lane_masks_bf = [m.astype(F32).astype(BF16) for m in lane_masks]
    q_chunk = lax.broadcasted_iota(jnp.int32, (ATT_BLOCK, ATT_KEYS), 0) // CHUNK
    key_idx = lax.broadcasted_iota(jnp.int32, (ATT_BLOCK, ATT_KEYS), 1)
    key_chunk = key_idx // CHUNK
    band = (key_chunk >= q_chunk) & (key_chunk <= q_chunk + SWA_WINDOW // CHUNK)
    has_prev = jnp.where(t > 0, SWA_WINDOW, 0)
    band_first = band & (key_idx + has_prev >= SWA_WINDOW)
    for i in range(T // ATT_BLOCK):
        rows = slice(i * ATT_BLOCK, (i + 1) * ATT_BLOCK)
        krows = slice(i * ATT_BLOCK, i * ATT_BLOCK + ATT_KEYS)
        mask = band_first if i == 0 else band
        for g in range(Q_GROUP):
            gcols = slice(g * KV_WIDTH, (g + 1) * KV_WIDTH)
            sinks = [sinks_ref[h * Q_GROUP + g] for h in range(KV_HEADS)]
            o = _sink_softmax_attention(q_ref[rows, gcols], kext_ref[krows, :], vext_ref[krows, :],
                                        mask, sinks, ATT_BLOCK, lane_masks, lane_masks_bf)
            b_ref[rows, gcols] = o.astype(BF16)

    h_ref[...] = _merge_and_residual(x, xn_ref, a_ref, b_ref, win_ref, wa_ref, wb_ref, wo_ref, gpost_ref[...])


def _mixer_sample_kernel(sinks_ref, x_ref, cos_ref, sup_ref, sdn_ref, gpre_ref, gpost_ref, win_ref,
                         lng_ref, lnb_ref, sguw_ref, sgub_ref, wa_ref, wb_ref, wo_ref,
                         ck_ref, cv_ref,
                         h_ref, knew_ref, vnew_ref, vsgu_ref,
                         xn_ref, vn_ref, a_ref, q_ref, kn_ref, vnew_bf_ref, b_ref,
                         *, n_seq, seq_len):
    x = x_ref[...]
    xn_ref[...] = _rmsnorm(x, gpre_ref[...]).astype(BF16)

    vn = _layernorm(_gelu(_dot(xn_ref[...], win_ref[:, OFF_V:OFF_V + D_MODEL])), lng_ref[...], lnb_ref[...])
    vsgu_ref[...] = vn
    vn_ref[...] = vn.astype(BF16)
    gu = _gelu(_dot(xn_ref[...], win_ref[:, OFF_U:OFF_U + D_MODEL]))
    w_sgu = _masked_sgu_weights(sguw_ref, seq_len)
    for g in range(SGU_GROUPS):
        gcols = slice(g * SGU_GROUP_DIM, (g + 1) * SGU_GROUP_DIM)
        rhs = jnp.concatenate([vn_ref[n * seq_len:(n + 1) * seq_len, gcols] for n in range(n_seq)], axis=1)
        s = _dot(w_sgu[g], rhs)
        bias = sgub_ref[0:seq_len, gcols]
        for n in range(n_seq):
            rows = slice(n * seq_len, (n + 1) * seq_len)
            sn = s[:, n * SGU_GROUP_DIM:(n + 1) * SGU_GROUP_DIM] + bias
            a_ref[rows, gcols] = (gu[rows, gcols] * sn).astype(BF16)

    cos, sup, sdn = cos_ref[...], sup_ref[...], sdn_ref[...]
    q = _rope(_dot(xn_ref[...], win_ref[:, OFF_Q:OFF_Q + Q_WIDTH]), cos, sup, sdn)
    q_ref[...] = (q * (HEAD_DIM ** -0.5)).astype(BF16)
    k = _rope(_dot(xn_ref[...], win_ref[:, OFF_K:OFF_K + KV_WIDTH]), cos, sup, sdn)
    va = _dot(xn_ref[...], win_ref[:, OFF_VA:OFF_VA + KV_WIDTH])
    knew_ref[...] = k
    vnew_ref[...] = va
    kn_ref[...] = k.astype(BF16)
    vnew_bf_ref[...] = va.astype(BF16)

    n_keys = SWA_WINDOW + seq_len
    pad = jnp.zeros((ATT_KEYS - n_keys, KV_WIDTH), BF16)
    lane_masks = _head_lane_masks(seq_len)
    lane_masks_bf = [m.astype(F32).astype(BF16) for m in lane_masks]
    mask = lax.broadcasted_iota(jnp.int32, (seq_len, ATT_KEYS), 1) < n_keys

    def per_sequence(n, carry):
        new_rows = pl.ds(pl.multiple_of(n * seq_len, seq_len), seq_len)
        old_rows = pl.ds(pl.multiple_of(n * SWA_WINDOW, SWA_WINDOW), SWA_WINDOW)
        k_win = jnp.concatenate([ck_ref[old_rows, :].astype(BF16), kn_ref[new_rows, :], pad], axis=0)
        v_win = jnp.concatenate([cv_ref[old_rows, :].astype(BF16), vnew_bf_ref[new_rows, :], pad], axis=0)
        for g in range(Q_GROUP):
            gcols = slice(g * KV_WIDTH, (g + 1) * KV_WIDTH)
            sinks = [sinks_ref[h * Q_GROUP + g] for h in range(KV_HEADS)]
            o = _sink_softmax_attention(q_ref[new_rows, gcols], k_win, v_win, mask, sinks, seq_len,
                                        lane_masks, lane_masks_bf)
            b_ref[new_rows, gcols] = o.astype(BF16)
        return carry

    lax.fori_loop(0, n_seq, per_sequence, 0)

    h_ref[...] = _merge_and_residual(x, xn_ref, a_ref, b_ref, win_ref, wa_ref, wb_ref, wo_ref, gpost_ref[...])


def _ffn_kernel(h_ref, gpre_ref, gpost_ref, w1_ref, w2_ref, y_ref, hn_ref):
    n_sub = max(1, h_ref.shape[0] // FFN_SUB)
    sub = h_ref.shape[0] // n_sub
    for r in range(n_sub):
        rows = slice(r * sub, (r + 1) * sub)
        h = h_ref[rows]
        hn_ref[rows] = _rmsnorm(h, gpre_ref[...]).astype(BF16)
        z = None
        for j in range(D_FF // FF_CHUNK):
            f = _dot(hn_ref[rows], w1_ref[:, j * FF_CHUNK:(j + 1) * FF_CHUNK])
            f = jnp.square(jnp.maximum(f, 0.0)).astype(BF16)
            zj = _dot(f, w2_ref[j * FF_CHUNK:(j + 1) * FF_CHUNK, :])
            z = zj if z is None else z + zj
        y_ref[rows] = h + _rmsnorm(z, gpost_ref[...])


def _resident(shape):
    return pl.BlockSpec(shape, lambda *_: (0,) * len(shape), pipeline_mode=pl.Buffered(1))


def _rope_tables(pos):
    half = ROT_DIM // 2
    inv = ROPE_THETA ** (-jnp.arange(half, dtype=F32) * 2.0 / ROT_DIM)
    ang = pos[:, None] * inv[None, :]
    cos, sin = jnp.cos(ang), jnp.sin(ang)
    n = pos.shape[0]
    zeros = lambda w: jnp.zeros((n, w), F32)
    cos_t = jnp.concatenate([cos, cos, jnp.ones((n, HEAD_DIM - ROT_DIM), F32)], axis=1)
    sup_t = jnp.concatenate([zeros(half), sin, zeros(HEAD_DIM - ROT_DIM)], axis=1)
    sdn_t = jnp.concatenate([-sin, zeros(HEAD_DIM - half)], axis=1)
    tile = lambda t: jnp.tile(t, (1, LANES // HEAD_DIM))
    return tile(cos_t), tile(sup_t), tile(sdn_t)


def _q_heads_by_group(w, axis):
    shape = w.shape
    split = shape[:axis] + (KV_HEADS, Q_GROUP, HEAD_DIM) + shape[axis + 1:]
    return jnp.swapaxes(w.reshape(split), axis, axis + 1).reshape(shape)


def _shared_mixer_operands(w):
    specs = [
        _resident((1, D_MODEL)), _resident((1, D_MODEL)), _resident((D_MODEL, IN_WIDTH)),
        _resident((1, D_MODEL)), _resident((1, D_MODEL)),
        _resident((SGU_GROUPS, SGU_CHUNK, SGU_CHUNK)), _resident((SGU_CHUNK, D_MODEL)),
        _resident((D_MODEL, D_MODEL)), _resident((Q_WIDTH, D_MODEL)), _resident((D_MODEL, D_MODEL)),
    ]
    args = [w["g_mix_pre"], w["g_mix_post"], w["w_in"], w["sgu_ln_g"], w["sgu_ln_b"], w["sgu_w"],
            w["sgu_b_rows"], w["w_branch_a"], w["w_branch_b"], w["w_out"]]
    return specs, args


def _mixer_prompt(x, w):
    B, S, _ = x.shape
    T = MIX_TILE
    cos, sup, sdn = _rope_tables(jnp.arange(S, dtype=F32))
    table_spec = pl.BlockSpec((T, LANES), lambda b, t: (t, 0))
    shared_specs, shared_args = _shared_mixer_operands(w)
    last_spec = pl.BlockSpec((None, SWA_WINDOW, KV_WIDTH), lambda b, t: (b, 0, 0))
    return pl.pallas_call(
        _mixer_prompt_kernel,
        grid=(B, S // T),
        in_specs=[pl.BlockSpec(memory_space=pltpu.SMEM),
                  pl.BlockSpec((None, T, D_MODEL), lambda b, t: (b, t, 0)),
                  table_spec, table_spec, table_spec] + shared_specs,
        out_specs=[pl.BlockSpec((None, T, D_MODEL), lambda b, t: (b, t, 0)), last_spec, last_spec],
        out_shape=[jax.ShapeDtypeStruct((B, S, D_MODEL), F32),
                   jax.ShapeDtypeStruct((B, SWA_WINDOW, KV_WIDTH), F32),
                   jax.ShapeDtypeStruct((B, SWA_WINDOW, KV_WIDTH), F32)],
        scratch_shapes=[
            pltpu.VMEM((T, D_MODEL), BF16),
            pltpu.VMEM((T, D_MODEL), F32),
            pltpu.VMEM((T, D_MODEL), BF16),
            pltpu.VMEM((T, D_MODEL), BF16),
            pltpu.VMEM((T, Q_WIDTH), BF16),
            pltpu.VMEM((T + SWA_WINDOW, KV_WIDTH), BF16),
            pltpu.VMEM((T + SWA_WINDOW, KV_WIDTH), BF16),
            pltpu.VMEM((T, Q_WIDTH), BF16),
        ],
        compiler_params=pltpu.CompilerParams(
            dimension_semantics=("arbitrary", "arbitrary"), vmem_limit_bytes=VMEM_LIMIT_BYTES),
        name="mixer_prompt",
    )(w["attn_sinks"], x, cos, sup, sdn, *shared_args)


def _mixer_sample(x, cache_k, cache_v, w):
    n_seq, seq_len, _ = x.shape
    rows = n_seq * seq_len
    cos, sup, sdn = _rope_tables(PAST_LEN + jnp.arange(seq_len, dtype=F32))
    tables = [jnp.tile(t, (n_seq, 1)) for t in (cos, sup, sdn)]
    shared_specs, shared_args = _shared_mixer_operands(w)
    cache_rows = n_seq * SWA_WINDOW
    return pl.pallas_call(
        functools.partial(_mixer_sample_kernel, n_seq=n_seq, seq_len=seq_len),
        grid=(1,),
        in_specs=[pl.BlockSpec(memory_space=pltpu.SMEM), _resident((rows, D_MODEL)),
                  _resident((rows, LANES)), _resident((rows, LANES)), _resident((rows, LANES))]
                 + shared_specs + [_resident((cache_rows, KV_WIDTH)), _resident((cache_rows, KV_WIDTH))],
        out_specs=[_resident((rows, D_MODEL)), _resident((rows, KV_WIDTH)), _resident((rows, KV_WIDTH)),
                   _resident((rows, D_MODEL))],
        out_shape=[jax.ShapeDtypeStruct((rows, D_MODEL), F32),
                   jax.ShapeDtypeStruct((rows, KV_WIDTH), F32),
                   jax.ShapeDtypeStruct((rows, KV_WIDTH), F32),
                   jax.ShapeDtypeStruct((rows, D_MODEL), F32)],
        scratch_shapes=[
            pltpu.VMEM((rows, D_MODEL), BF16),
            pltpu.VMEM((rows, D_MODEL), BF16),
            pltpu.VMEM((rows, D_MODEL), BF16),
            pltpu.VMEM((rows, Q_WIDTH), BF16),
            pltpu.VMEM((rows, KV_WIDTH), BF16),
            pltpu.VMEM((rows, KV_WIDTH), BF16),
            pltpu.VMEM((rows, Q_WIDTH), BF16),
        ],
        compiler_params=pltpu.CompilerParams(
            dimension_semantics=("arbitrary",), vmem_limit_bytes=VMEM_LIMIT_BYTES),
        name="mixer_sample",
    )(w["attn_sinks"], x.reshape(rows, D_MODEL), *tables, *shared_args,
      cache_k.reshape(cache_rows, KV_WIDTH), cache_v.reshape(cache_rows, KV_WIDTH))


def _ffn(h, w, tile, name):
    rows = h.shape[0]
    return pl.pallas_call(
        _ffn_kernel,
        grid=(rows // tile,),
        in_specs=[pl.BlockSpec((tile, D_MODEL), lambda i: (i, 0)),
                  _resident((1, D_MODEL)), _resident((1, D_MODEL)),
                  _resident((D_MODEL, D_FF)), _resident((D_FF, D_MODEL))],
        out_specs=pl.BlockSpec((tile, D_MODEL), lambda i: (i, 0)),
        out_shape=jax.ShapeDtypeStruct((rows, D_MODEL), F32),
        scratch_shapes=[pltpu.VMEM((tile, D_MODEL), BF16)],
        compiler_params=pltpu.CompilerParams(
            dimension_semantics=("arbitrary",), vmem_limit_bytes=VMEM_LIMIT_BYTES),
        name=name,
    )(h, w["g_ffn_pre"], w["g_ffn_post"], w["w_ff1"], w["w_ff2"])


def _layer(h_p, h_s, ck, cv, w):
    B, S, _ = h_p.shape
    n_seq, seq_len, _ = h_s.shape
    h_p, k_last, v_last = _mixer_prompt(h_p, w)
    y_p = _ffn(h_p.reshape(B * S, D_MODEL), w, FFN_TILE, "ffn_prompt").reshape(B, S, D_MODEL)
    h_s, k_new, v_new, v_sgu = _mixer_sample(h_s, ck, cv, w)
    y_s = _ffn(h_s, w, n_seq * seq_len, "ffn_sample").reshape(n_seq, seq_len, D_MODEL)
    return (y_p, y_s,
            k_last.reshape(B, SWA_WINDOW, KV_HEADS, HEAD_DIM), v_last.reshape(B, SWA_WINDOW, KV_HEADS, HEAD_DIM),
            k_new.reshape(n_seq, seq_len, KV_HEADS, HEAD_DIM), v_new.reshape(n_seq, seq_len, KV_HEADS, HEAD_DIM),
            v_sgu.reshape(n_seq, seq_len, D_MODEL))


def kernel(x_prompt, x_sample, cache_swa_k, cache_swa_v, w_in, sgu_ln_g, sgu_ln_b, sgu_w, sgu_b, attn_sinks,
           w_branch_a, w_branch_b, w_out, g_mix_pre, g_mix_post, g_ffn_pre, g_ffn_post, w_ff1, w_ff2):
    depth = w_in.shape[0]
    h_p, h_s = x_prompt, x_sample
    per_layer = []
    for l in range(depth):
        row = lambda v: v[l].reshape(1, -1)
        w_in_l = w_in[l].astype(BF16)
        w_in_l = jnp.concatenate([w_in_l[:, :OFF_Q], _q_heads_by_group(w_in_l[:, OFF_Q:OFF_K], 1),
                                  w_in_l[:, OFF_K:]], axis=1)
        w = {
            "w_in": w_in_l, "w_branch_a": w_branch_a[l].astype(BF16),
            "w_branch_b": _q_heads_by_group(w_branch_b[l].astype(BF16), 0), "w_out": w_out[l].astype(BF16),
            "w_ff1": w_ff1[l].astype(BF16), "w_ff2": w_ff2[l].astype(BF16),
            "sgu_ln_g": row(sgu_ln_g), "sgu_ln_b": row(sgu_ln_b), "sgu_w": sgu_w[l],
            "sgu_b_rows": jnp.repeat(sgu_b[l].T, SGU_GROUP_DIM, axis=1),
            "attn_sinks": attn_sinks[l],
            "g_mix_pre": row(g_mix_pre), "g_mix_post": row(g_mix_post),
            "g_ffn_pre": row(g_ffn_pre), "g_ffn_post": row(g_ffn_post),
        }
        h_p, h_s, *states = _layer(h_p, h_s, cache_swa_k[l], cache_swa_v[l], w)
        per_layer.append(states)
    stacked = [jnp.stack([states[i] for states in per_layer]) for i in range(5)]
    return (h_p, h_s, *stacked)
```

```python
import functools

import numpy as np
import jax
import jax.numpy as jnp
from jax import lax
from jax.experimental import pallas as pl
from jax.experimental.pallas import tpu as pltpu

D_MODEL = 1024
CHUNK = 64
SGU_CHUNK = 128
SGU_GROUPS = 8
SGU_GROUP_DIM = D_MODEL // SGU_GROUPS
N_HEADS = 16
KV_HEADS = 4
HEAD_DIM = 64
Q_GROUP = N_HEADS // KV_HEADS
SWA_WINDOW = 128
ROT_DIM = HEAD_DIM // 4
ROPE_THETA = 500000.0
D_FF = 4 * D_MODEL
NORM_EPS = 1e-6
PAST_LEN = 4096
MASK_VALUE = -1e30

Q_WIDTH = N_HEADS * HEAD_DIM
KV_WIDTH = KV_HEADS * HEAD_DIM
OFF_U = 0
OFF_V = OFF_U + D_MODEL
OFF_Q = OFF_V + D_MODEL
OFF_K = OFF_Q + Q_WIDTH
OFF_VA = OFF_K + KV_WIDTH
OFF_GA = OFF_VA + KV_WIDTH
OFF_GB = OFF_GA + D_MODEL
IN_WIDTH = OFF_GB + D_MODEL

LANES = 128
MXU_COLS = 256
VMEM_LIMIT_BYTES = 52 * 1024 * 1024
MIX_TILE = 512
MIX_SUB = 512
FFN_TILE = 1024
FFN_SUB = 512
ATT_BLOCK = 2 * CHUNK
ATT_KEYS = ATT_BLOCK + SWA_WINDOW
FF_CHUNK = 1024
SAMPLE_UNROLL = 4

F32 = jnp.float32
BF16 = jnp.bfloat16
SQRT_HALF = np.sqrt(0.5).astype(np.float32)
LOG2_E = np.float32(np.log2(np.e))
SCORE_SCALE = np.float32(HEAD_DIM ** -0.5) * LOG2_E
N_ROPE_TABLES = 6


def _dot(a, b):
    return jnp.dot(a, b, preferred_element_type=F32)


def _dot_nt(a, b):
    return lax.dot_general(a, b, (((1,), (1,)), ((), ())), preferred_element_type=F32)


def _rmsnorm(x, g):
    return x * lax.rsqrt(jnp.mean(x * x, axis=-1, keepdims=True) + NORM_EPS) * g


def _layernorm(x, g, b):
    xc = x - jnp.mean(x, axis=-1, keepdims=True)
    return xc * lax.rsqrt(jnp.mean(xc * xc, axis=-1, keepdims=True) + NORM_EPS) * g + b


def _gelu(x):
    return 0.5 * x * (1.0 + lax.erf(x * SQRT_HALF))


def _rope(x, cos, sin_up, sin_dn):
    out = []
    for j in range(x.shape[1] // LANES):
        xj = x[:, j * LANES:(j + 1) * LANES]
        up = pltpu.roll(xj, ROT_DIM // 2, axis=1)
        dn = pltpu.roll(xj, LANES - ROT_DIM // 2, axis=1)
        out.append(xj * cos + up * sin_up + dn * sin_dn)
    return jnp.concatenate(out, axis=1)


def _split_rope_tables(rope):
    tables = [rope[:, j * LANES:(j + 1) * LANES] for j in range(N_ROPE_TABLES)]
    return tables[:N_ROPE_TABLES // 2], tables[N_ROPE_TABLES // 2:]


def _head_lane_masks_bf16():
    lane = np.arange(KV_WIDTH)
    masks = np.stack([(lane // HEAD_DIM == h) for h in range(KV_HEADS)]).astype(np.float32)
    return jnp.asarray(np.broadcast_to(masks[:, None, :], (KV_HEADS, ATT_BLOCK, KV_WIDTH)), dtype=BF16)


def _masked_sgu_weights(sguw_ref, size):
    i = lax.broadcasted_iota(jnp.int32, (size, size), 0) // CHUNK
    j = lax.broadcasted_iota(jnp.int32, (size, size), 1) // CHUNK
    keep = i >= j
    return [jnp.where(keep, sguw_ref[g, :size, :size], 0.0).astype(BF16) for g in range(SGU_GROUPS)]


def _stack_heads(q_bf, lane_masks_bf):
    return jnp.concatenate([q_bf * lane_masks_bf[h] for h in range(KV_HEADS)], axis=0)


def _band_mask(s):
    half = ATT_KEYS // 2
    lane = lax.broadcasted_iota(jnp.int32, (CHUNK, half), 1)
    top = jnp.concatenate([s[:CHUNK, :half], jnp.where(lane < CHUNK, s[:CHUNK, half:], MASK_VALUE)], axis=1)
    bot = jnp.concatenate([jnp.where(lane >= CHUNK, s[CHUNK:, :half], MASK_VALUE), s[CHUNK:, half:]], axis=1)
    return jnp.concatenate([top, bot], axis=0)


def _attention_probs(q_stack, k_win, mask, sinks, rows):
    s = _dot_nt(q_stack, k_win)
    es, invs = [], []
    for h in range(KV_HEADS):
        sh = mask(s[h * rows:(h + 1) * rows])
        m = jnp.maximum(jnp.max(sh, axis=-1, keepdims=True), sinks[h])
        e = jnp.exp2(sh - m)
        den = jnp.sum(e, axis=-1, keepdims=True) + jnp.exp2(sinks[h] - m)
        es.append(e.astype(BF16))
        invs.append(1.0 / den)
    return jnp.concatenate(es, axis=0), invs


def _attention_output(e_stack, invs, v_win, rows):
    o = _dot(e_stack, v_win)
    lane = lax.broadcasted_iota(jnp.int32, (rows, KV_WIDTH), 1)
    out = o[(KV_HEADS - 1) * rows:] * invs[KV_HEADS - 1]
    for h in range(KV_HEADS - 2, -1, -1):
        out = jnp.where(lane < (h + 1) * HEAD_DIM, o[h * rows:(h + 1) * rows] * invs[h], out)
    return out


def _merge_and_residual(x, xn_ref, a_ref, b_ref, win_ref, wa_ref, wb_ref, wo_ref, gpost):
    ga = jax.nn.sigmoid(_dot(xn_ref[...], win_ref[:, OFF_GA:OFF_GA + D_MODEL]))
    m = ga * _dot(a_ref[...], wa_ref[...])
    gb = jax.nn.sigmoid(_dot(xn_ref[...], win_ref[:, OFF_GB:OFF_GB + D_MODEL]))
    m = m + gb * _dot(b_ref[...], wb_ref[...])
    out = _dot(m.astype(BF16), wo_ref[...])
    return x + _rmsnorm(out, gpost)


def _mixer_prompt_kernel(sinks_ref, x_ref, rope_ref, gpre_ref, gpost_ref, win_ref,
                         lng_ref, lnb_ref, sguw_ref, sgub_ref, qmask_ref, wa_ref, wb_ref, wo_ref,
                         h_ref, klast_ref, vlast_ref,
                         xn_ref, gv_ref, vn_ref, a_ref, q_ref, kext_ref, vext_ref, b_ref):
    T = x_ref.shape[0]
    t = pl.program_id(1)

    @pl.when(t == 0)
    def _():
        kext_ref[0:SWA_WINDOW] = jnp.zeros((SWA_WINDOW, KV_WIDTH), BF16)
        vext_ref[0:SWA_WINDOW] = jnp.zeros((SWA_WINDOW, KV_WIDTH), BF16)

    @pl.when(t > 0)
    def _():
        kext_ref[0:SWA_WINDOW] = kext_ref[T:T + SWA_WINDOW]
        vext_ref[0:SWA_WINDOW] = vext_ref[T:T + SWA_WINDOW]

    w_sgu = _masked_sgu_weights(sguw_ref, SGU_CHUNK)
    lane_masks_bf = [qmask_ref[h] for h in range(KV_HEADS)]
    q_chunk = lax.broadcasted_iota(jnp.int32, (ATT_BLOCK, ATT_KEYS), 0) // CHUNK
    key_idx = lax.broadcasted_iota(jnp.int32, (ATT_BLOCK, ATT_KEYS), 1)
    key_chunk = key_idx // CHUNK
    band = (key_chunk >= q_chunk) & (key_chunk <= q_chunk + SWA_WINDOW // CHUNK)
    has_prev = jnp.where(t > 0, SWA_WINDOW, 0)
    band_first = band & (key_idx + has_prev >= SWA_WINDOW)
    first_block_mask = lambda s: jnp.where(band_first, s, MASK_VALUE)

    def norm_in(r0):
        rows = slice(r0, r0 + MIX_SUB)
        xn_ref[rows] = _rmsnorm(x_ref[rows], gpre_ref[...]).astype(BF16)

    def gelu_v(r0):
        rows = slice(r0, r0 + MIX_SUB)
        for c in range(D_MODEL // MXU_COLS):
            cols = slice(c * MXU_COLS, (c + 1) * MXU_COLS)
            gv_ref[rows, cols] = _gelu(
                _dot(xn_ref[rows], win_ref[:, OFF_V + c * MXU_COLS:OFF_V + (c + 1) * MXU_COLS]))

    def attention_inputs(r0):
        rows = slice(r0, r0 + MIX_SUB)
        k_tables, q_tables = _split_rope_tables(rope_ref[rows])
        q_ref[rows] = _rope(_dot(xn_ref[rows], win_ref[:, OFF_Q:OFF_Q + Q_WIDTH]), *q_tables).astype(BF16)
        kva = _dot(xn_ref[rows], win_ref[:, OFF_K:OFF_VA + KV_WIDTH])
        k = _rope(kva[:, :KV_WIDTH], *k_tables)
        va = kva[:, KV_WIDTH:]
        if r0 + MIX_SUB == T:
            klast_ref[...] = k[MIX_SUB - SWA_WINDOW:]
            vlast_ref[...] = va[MIX_SUB - SWA_WINDOW:]
        kext_ref[SWA_WINDOW + r0:SWA_WINDOW + r0 + MIX_SUB] = k.astype(BF16)
        vext_ref[SWA_WINDOW + r0:SWA_WINDOW + r0 + MIX_SUB] = va.astype(BF16)

    def norm_v(r0):
        rows = slice(r0, r0 + MIX_SUB)
        vn_ref[rows] = _layernorm(gv_ref[rows], lng_ref[...], lnb_ref[...]).astype(BF16)

    def gating(r0):
        n_chunks = MIX_SUB // SGU_CHUNK
        groups_per_step = MXU_COLS // SGU_GROUP_DIM
        for c in range(D_MODEL // MXU_COLS):
            gu = _gelu(_dot(xn_ref[r0:r0 + MIX_SUB],
                            win_ref[:, OFF_U + c * MXU_COLS:OFF_U + (c + 1) * MXU_COLS]))
            for gg in range(groups_per_step):
                g = c * groups_per_step + gg
                gcols = slice(g * SGU_GROUP_DIM, (g + 1) * SGU_GROUP_DIM)
                rhs = jnp.concatenate(
                    [vn_ref[r0 + n * SGU_CHUNK:r0 + (n + 1) * SGU_CHUNK, gcols] for n in range(n_chunks)], axis=1)
                s = _dot(w_sgu[g], rhs)
                bias = sgub_ref[:, gcols]
                for n in range(n_chunks):
                    sn = s[:, n * SGU_GROUP_DIM:(n + 1) * SGU_GROUP_DIM] + bias
                    gun = gu[n * SGU_CHUNK:(n + 1) * SGU_CHUNK, gg * SGU_GROUP_DIM:(gg + 1) * SGU_GROUP_DIM]
                    a_ref[r0 + n * SGU_CHUNK:r0 + (n + 1) * SGU_CHUNK, gcols] = (gun * sn).astype(BF16)

    def attention(r0):
        blocks = [(i, g) for i in range(r0 // ATT_BLOCK, (r0 + MIX_SUB) // ATT_BLOCK) for g in range(Q_GROUP)]

        def probs(i, g):
            rows = slice(i * ATT_BLOCK, (i + 1) * ATT_BLOCK)
            krows = slice(i * ATT_BLOCK, i * ATT_BLOCK + ATT_KEYS)
            sinks = [sinks_ref[h * Q_GROUP + g] * LOG2_E for h in range(KV_HEADS)]
            q_stack = _stack_heads(q_ref[rows, g * KV_WIDTH:(g + 1) * KV_WIDTH], lane_masks_bf)
            mask = first_block_mask if i == 0 else _band_mask
            return _attention_probs(q_stack, kext_ref[krows, :], mask, sinks, ATT_BLOCK)

        def output(i, g, e_stack, invs):
            rows = slice(i * ATT_BLOCK, (i + 1) * ATT_BLOCK)
            krows = slice(i * ATT_BLOCK, i * ATT_BLOCK + ATT_KEYS)
            o = _attention_output(e_stack, invs, vext_ref[krows, :], ATT_BLOCK)
            b_ref[rows, g * KV_WIDTH:(g + 1) * KV_WIDTH] = o.astype(BF16)

        for block in blocks:
            output(*block, *probs(*block))

    def merge(r0):
        rows = slice(r0, r0 + MIX_SUB)
        ga = jax.nn.sigmoid(_dot(xn_ref[rows], win_ref[:, OFF_GA:OFF_GA + D_MODEL]))
        m = ga * _dot(a_ref[rows], wa_ref[...])
        gb = jax.nn.sigmoid(_dot(xn_ref[rows], win_ref[:, OFF_GB:OFF_GB + D_MODEL]))
        m = m + gb * _dot(b_ref[rows], wb_ref[...])
        out = _dot(m.astype(BF16), wo_ref[...])
        h_ref[rows] = x_ref[rows] + _rmsnorm(out, gpost_ref[...])

    for phase in (norm_in, gelu_v, norm_v, gating, attention_inputs, attention, merge):
        for r0 in range(0, T, MIX_SUB):
            phase(r0)


def _mixer_sample_kernel(sinks_ref, x_ref, rope_ref, gpre_ref, gpost_ref, win_ref,
                         lng_ref, lnb_ref, sguw_ref, sgub_ref, qmask_ref, wa_ref, wb_ref, wo_ref,
                         ck_ref, cv_ref,
                         h_ref, knew_ref, vnew_ref, vsgu_ref,
                         xn_ref, vn_ref, a_ref, q_ref, kn_ref, vnew_bf_ref, b_ref,
                         *, n_seq, seq_len):
    x = x_ref[...]
    xn_ref[...] = _rmsnorm(x, gpre_ref[...]).astype(BF16)

    vn = _layernorm(_gelu(_dot(xn_ref[...], win_ref[:, OFF_V:OFF_V + D_MODEL])), lng_ref[...], lnb_ref[...])
    vsgu_ref[...] = vn
    vn_ref[...] = vn.astype(BF16)
    gu = _gelu(_dot(xn_ref[...], win_ref[:, OFF_U:OFF_U + D_MODEL]))
    w_sgu = _masked_sgu_weights(sguw_ref, seq_len)
    for g in range(SGU_GROUPS):
        gcols = slice(g * SGU_GROUP_DIM, (g + 1) * SGU_GROUP_DIM)
        rhs = jnp.concatenate([vn_ref[n * seq_len:(n + 1) * seq_len, gcols] for n in range(n_seq)], axis=1)
        s = _dot(w_sgu[g], rhs)
        bias = sgub_ref[0:seq_len, gcols]
        for n in range(n_seq):
            rows = slice(n * seq_len, (n + 1) * seq_len)
            sn = s[:, n * SGU_GROUP_DIM:(n + 1) * SGU_GROUP_DIM] + bias
            a_ref[rows, gcols] = (gu[rows, gcols] * sn).astype(BF16)

    k_tables, q_tables = _split_rope_tables(rope_ref[...])
    q_ref[...] = _rope(_dot(xn_ref[...], win_ref[:, OFF_Q:OFF_Q + Q_WIDTH]), *q_tables).astype(BF16)
    k = _rope(_dot(xn_ref[...], win_ref[:, OFF_K:OFF_K + KV_WIDTH]), *k_tables)
    va = _dot(xn_ref[...], win_ref[:, OFF_VA:OFF_VA + KV_WIDTH])
    knew_ref[...] = k
    vnew_ref[...] = va
    kn_ref[...] = k.astype(BF16)
    vnew_bf_ref[...] = va.astype(BF16)

    n_keys = SWA_WINDOW + seq_len
    pad = jnp.zeros((ATT_KEYS - n_keys, KV_WIDTH), BF16)
    lane_masks_bf = [qmask_ref[h, 0:seq_len, :] for h in range(KV_HEADS)]
    real_key = lax.broadcasted_iota(jnp.int32, (seq_len, ATT_KEYS), 1) < n_keys
    mask = lambda s: jnp.where(real_key, s, MASK_VALUE)

    def per_sequence(n, carry):
        new_rows = pl.ds(pl.multiple_of(n * seq_len, seq_len), seq_len)
        old_rows = pl.ds(pl.multiple_of(n * SWA_WINDOW, SWA_WINDOW), SWA_WINDOW)
        k_win = jnp.concatenate([ck_ref[old_rows, :].astype(BF16), kn_ref[new_rows, :], pad], axis=0)
        v_win = jnp.concatenate([cv_ref[old_rows, :].astype(BF16), vnew_bf_ref[new_rows, :], pad], axis=0)
        for g in range(Q_GROUP):
            gcols = slice(g * KV_WIDTH, (g + 1) * KV_WIDTH)
            sinks = [sinks_ref[h * Q_GROUP + g] * LOG2_E for h in range(KV_HEADS)]
            e_stack, invs = _attention_probs(_stack_heads(q_ref[new_rows, gcols], lane_masks_bf), k_win, mask,
                                             sinks, seq_len)
            b_ref[new_rows, gcols] = _attention_output(e_stack, invs, v_win, seq_len).astype(BF16)
        return carry

    lax.fori_loop(0, n_seq, per_sequence, 0, unroll=SAMPLE_UNROLL)

    h_ref[...] = _merge_and_residual(x, xn_ref, a_ref, b_ref, win_ref, wa_ref, wb_ref, wo_ref, gpost_ref[...])


def _ffn_kernel(h_ref, gpre_ref, gpost_ref, w1_ref, w2_ref, y_ref, hn_ref):
    n_sub = max(1, h_ref.shape[0] // FFN_SUB)
    sub = h_ref.shape[0] // n_sub
    for r in range(n_sub):
        rows = slice(r * sub, (r + 1) * sub)
        h = h_ref[rows]
        hn_ref[rows] = _rmsnorm(h, gpre_ref[...]).astype(BF16)
        z = None
        for j in range(D_FF // FF_CHUNK):
            f = _dot(hn_ref[rows], w1_ref[:, j * FF_CHUNK:(j + 1) * FF_CHUNK])
            f = jnp.square(jnp.maximum(f, 0.0)).astype(BF16)
            zj = _dot(f, w2_ref[j * FF_CHUNK:(j + 1) * FF_CHUNK, :])
            z = zj if z is None else z + zj
        y_ref[rows] = h + _rmsnorm(z, gpost_ref[...])


def _resident(shape):
    return pl.BlockSpec(shape, lambda *_: (0,) * len(shape), pipeline_mode=pl.Buffered(1))


def _rope_tables(pos):
    half = ROT_DIM // 2
    inv = ROPE_THETA ** (-jnp.arange(half, dtype=F32) * 2.0 / ROT_DIM)
    ang = pos[:, None] * inv[None, :]
    cos, sin = jnp.cos(ang), jnp.sin(ang)
    n = pos.shape[0]
    zeros = lambda w: jnp.zeros((n, w), F32)
    cos_t = jnp.concatenate([cos, cos, jnp.ones((n, HEAD_DIM - ROT_DIM), F32)], axis=1)
    sup_t = jnp.concatenate([zeros(half), sin, zeros(HEAD_DIM - ROT_DIM)], axis=1)
    sdn_t = jnp.concatenate([-sin, zeros(HEAD_DIM - half)], axis=1)
    tile = lambda t: jnp.tile(t, (1, LANES // HEAD_DIM))
    k_tables = [tile(cos_t), tile(sup_t), tile(sdn_t)]
    return jnp.concatenate(k_tables + [t * SCORE_SCALE for t in k_tables], axis=1)


def _q_heads_by_group(w, axis):
    shape = w.shape
    split = shape[:axis] + (KV_HEADS, Q_GROUP, HEAD_DIM) + shape[axis + 1:]
    return jnp.swapaxes(w.reshape(split), axis, axis + 1).reshape(shape)


def _shared_mixer_operands(w):
    specs = [
        _resident((1, D_MODEL)), _resident((1, D_MODEL)), _resident((D_MODEL, IN_WIDTH)),
        _resident((1, D_MODEL)), _resident((1, D_MODEL)),
        _resident((SGU_GROUPS, SGU_CHUNK, SGU_CHUNK)), _resident((SGU_CHUNK, D_MODEL)),
        _resident((KV_HEADS, ATT_BLOCK, KV_WIDTH)),
        _resident((D_MODEL, D_MODEL)), _resident((Q_WIDTH, D_MODEL)), _resident((D_MODEL, D_MODEL)),
    ]
    args = [w["g_mix_pre"], w["g_mix_post"], w["w_in"], w["sgu_ln_g"], w["sgu_ln_b"], w["sgu_w"],
            w["sgu_b_rows"], _head_lane_masks_bf16(), w["w_branch_a"], w["w_branch_b"], w["w_out"]]
    return specs, args


def _mixer_prompt(x, w):
    B, S, _ = x.shape
    T = MIX_TILE
    rope = _rope_tables(jnp.arange(S, dtype=F32))
    table_spec = pl.BlockSpec((T, N_ROPE_TABLES * LANES), lambda b, t: (t, 0))
    shared_specs, shared_args = _shared_mixer_operands(w)
    last_spec = pl.BlockSpec((None, SWA_WINDOW, KV_WIDTH), lambda b, t: (b, 0, 0))
    return pl.pallas_call(
        _mixer_prompt_kernel,
        grid=(B, S // T),
        in_specs=[pl.BlockSpec(memory_space=pltpu.SMEM),
                  pl.BlockSpec((None, T, D_MODEL), lambda b, t: (b, t, 0)),
                  table_spec] + shared_specs,
        out_specs=[pl.BlockSpec((None, T, D_MODEL), lambda b, t: (b, t, 0)), last_spec, last_spec],
        out_shape=[jax.ShapeDtypeStruct((B, S, D_MODEL), F32),
                   jax.ShapeDtypeStruct((B, SWA_WINDOW, KV_WIDTH), F32),
                   jax.ShapeDtypeStruct((B, SWA_WINDOW, KV_WIDTH), F32)],
        scratch_shapes=[
            pltpu.VMEM((T, D_MODEL), BF16),
            pltpu.VMEM((T, D_MODEL), F32),
            pltpu.VMEM((T, D_MODEL), BF16),
            pltpu.VMEM((T, D_MODEL), BF16),
            pltpu.VMEM((T, Q_WIDTH), BF16),
            pltpu.VMEM((T + SWA_WINDOW, KV_WIDTH), BF16),
            pltpu.VMEM((T + SWA_WINDOW, KV_WIDTH), BF16),
            pltpu.VMEM((T, Q_WIDTH), BF16),
        ],
        compiler_params=pltpu.CompilerParams(
            dimension_semantics=("arbitrary", "arbitrary"), vmem_limit_bytes=VMEM_LIMIT_BYTES),
        name="mixer_prompt",
    )(w["attn_sinks"], x, rope, *shared_args)


def _mixer_sample(x, cache_k, cache_v, w):
    n_seq, seq_len, _ = x.shape
    rows = n_seq * seq_len
    rope = jnp.tile(_rope_tables(PAST_LEN + jnp.arange(seq_len, dtype=F32)), (n_seq, 1))
    shared_specs, shared_args = _shared_mixer_operands(w)
    cache_rows = n_seq * SWA_WINDOW
    return pl.pallas_call(
        functools.partial(_mixer_sample_kernel, n_seq=n_seq, seq_len=seq_len),
        grid=(1,),
        in_specs=[pl.BlockSpec(memory_space=pltpu.SMEM), _resident((rows, D_MODEL)),
                  _resident((rows, N_ROPE_TABLES * LANES))]
                 + shared_specs + [_resident((cache_rows, KV_WIDTH)), _resident((cache_rows, KV_WIDTH))],
        out_specs=[_resident((rows, D_MODEL)), _resident((rows, KV_WIDTH)), _resident((rows, KV_WIDTH)),
                   _resident((rows, D_MODEL))],
        out_shape=[jax.ShapeDtypeStruct((rows, D_MODEL), F32),
                   jax.ShapeDtypeStruct((rows, KV_WIDTH), F32),
                   jax.ShapeDtypeStruct((rows, KV_WIDTH), F32),
                   jax.ShapeDtypeStruct((rows, D_MODEL), F32)],
        scratch_shapes=[
            pltpu.VMEM((rows, D_MODEL), BF16),
            pltpu.VMEM((rows, D_MODEL), BF16),
            pltpu.VMEM((rows, D_MODEL), BF16),
            pltpu.VMEM((rows, Q_WIDTH), BF16),
            pltpu.VMEM((rows, KV_WIDTH), BF16),
            pltpu.VMEM((rows, KV_WIDTH), BF16),
            pltpu.VMEM((rows, Q_WIDTH), BF16),
        ],
        compiler_params=pltpu.CompilerParams(
            dimension_semantics=("arbitrary",), vmem_limit_bytes=VMEM_LIMIT_BYTES),
        name="mixer_sample",
    )(w["attn_sinks"], x.reshape(rows, D_MODEL), rope, *shared_args,
      cache_k.reshape(cache_rows, KV_WIDTH), cache_v.reshape(cache_rows, KV_WIDTH))


def _ffn(h, w, tile, name):
    rows = h.shape[0]
    return pl.pallas_call(
        _ffn_kernel,
        grid=(rows // tile,),
        in_specs=[pl.BlockSpec((tile, D_MODEL), lambda i: (i, 0)),
                  _resident((1, D_MODEL)), _resident((1, D_MODEL)),
                  _resident((D_MODEL, D_FF)), _resident((D_FF, D_MODEL))],
        out_specs=pl.BlockSpec((tile, D_MODEL), lambda i: (i, 0)),
        out_shape=jax.ShapeDtypeStruct((rows, D_MODEL), F32),
        scratch_shapes=[pltpu.VMEM((tile, D_MODEL), BF16)],
        compiler_params=pltpu.CompilerParams(
            dimension_semantics=("arbitrary",), vmem_limit_bytes=VMEM_LIMIT_BYTES),
        name=name,
    )(h, w["g_ffn_pre"], w["g_ffn_post"], w["w_ff1"], w["w_ff2"])


def _layer(h_p, h_s, ck, cv, w):
    B, S, _ = h_p.shape
    n_seq, seq_len, _ = h_s.shape
    h_p, k_last, v_last = _mixer_prompt(h_p, w)
    y_p = _ffn(h_p.reshape(B * S, D_MODEL), w, FFN_TILE, "ffn_prompt").reshape(B, S, D_MODEL)
    h_s, k_new, v_new, v_sgu = _mixer_sample(h_s, ck, cv, w)
    y_s = _ffn(h_s, w, n_seq * seq_len, "ffn_sample").reshape(n_seq, seq_len, D_MODEL)
    return (y_p, y_s,
            k_last.reshape(B, SWA_WINDOW, KV_HEADS, HEAD_DIM), v_last.reshape(B, SWA_WINDOW, KV_HEADS, HEAD_DIM),
            k_new.reshape(n_seq, seq_len, KV_HEADS, HEAD_DIM), v_new.reshape(n_seq, seq_len, KV_HEADS, HEAD_DIM),
            v_sgu.reshape(n_seq, seq_len, D_MODEL))


def kernel(x_prompt, x_sample, cache_swa_k, cache_swa_v, w_in, sgu_ln_g, sgu_ln_b, sgu_w, sgu_b, attn_sinks,
           w_branch_a, w_branch_b, w_out, g_mix_pre, g_mix_post, g_ffn_pre, g_ffn_post, w_ff1, w_ff2):
    depth = w_in.shape[0]
    h_p, h_s = x_prompt, x_sample
    per_layer = []
    for l in range(depth):
        row = lambda v: v[l].reshape(1, -1)
        w_in_l = w_in[l].astype(BF16)
        w_in_l = jnp.concatenate([w_in_l[:, :OFF_Q], _q_heads_by_group(w_in_l[:, OFF_Q:OFF_K], 1),
                                  w_in_l[:, OFF_K:]], axis=1)
        w = {
            "w_in": w_in_l, "w_branch_a": w_branch_a[l].astype(BF16),
            "w_branch_b": _q_heads_by_group(w_branch_b[l].astype(BF16), 0), "w_out": w_out[l].astype(BF16),
            "w_ff1": w_ff1[l].astype(BF16), "w_ff2": w_ff2[l].astype(BF16),
            "sgu_ln_g": row(sgu_ln_g), "sgu_ln_b": row(sgu_ln_b), "sgu_w": sgu_w[l],
            "sgu_b_rows": jnp.repeat(sgu_b[l].T, SGU_GROUP_DIM, axis=1),
            "attn_sinks": attn_sinks[l],
            "g_mix_pre": row(g_mix_pre), "g_mix_post": row(g_mix_post),
            "g_ffn_pre": row(g_ffn_pre), "g_ffn_post": row(g_ffn_post),
        }
        h_p, h_s, *states = _layer(h_p, h_s, cache_swa_k[l], cache_swa_v[l], w)
        per_layer.append(states)
    stacked = [jnp.stack([states[i] for states in per_layer]) for i in range(5)]
    return (h_p, h_s, *stacked)
```

```python
import functools

import numpy as np
import jax
import jax.numpy as jnp
from jax import lax
from jax.experimental import pallas as pl
from jax.experimental.pallas import tpu as pltpu

D_MODEL = 1024
CHUNK = 64
SGU_CHUNK = 128
SGU_GROUPS = 8
SGU_GROUP_DIM = D_MODEL // SGU_GROUPS
N_HEADS = 16
KV_HEADS = 4
HEAD_DIM = 64
Q_GROUP = N_HEADS // KV_HEADS
SWA_WINDOW = 128
ROT_DIM = HEAD_DIM // 4
ROPE_THETA = 500000.0
D_FF = 4 * D_MODEL
NORM_EPS = 1e-6
PAST_LEN = 4096
MASK_VALUE = -1e30

Q_WIDTH = N_HEADS * HEAD_DIM
KV_WIDTH = KV_HEADS * HEAD_DIM
OFF_U = 0
OFF_V = OFF_U + D_MODEL
OFF_Q = OFF_V + D_MODEL
OFF_K = OFF_Q + Q_WIDTH
OFF_VA = OFF_K + KV_WIDTH
OFF_GA = OFF_VA + KV_WIDTH
OFF_GB = OFF_GA + D_MODEL
IN_WIDTH = OFF_GB + D_MODEL

LANES = 128
MXU_COLS = 256
VMEM_LIMIT_BYTES = 56 * 1024 * 1024
MIX_TILE = 512
MIX_SUB = 512
FFN_TILE = 2048
FFN_SUB = 512
ATT_BLOCK = 2 * CHUNK
ATT_KEYS = ATT_BLOCK + SWA_WINDOW
FF_CHUNK = 1024
SAMPLE_UNROLL = 4

F32 = jnp.float32
BF16 = jnp.bfloat16
SQRT_HALF = np.sqrt(0.5).astype(np.float32)
LOG2_E = np.float32(np.log2(np.e))
SCORE_SCALE = np.float32(HEAD_DIM ** -0.5) * LOG2_E
N_ROPE_TABLES = 4


def _dot(a, b):
    return jnp.dot(a, b, preferred_element_type=F32)


def _dot_nt(a, b):
    return lax.dot_general(a, b, (((1,), (1,)), ((), ())), preferred_element_type=F32)


def _rmsnorm(x, g):
    return x * lax.rsqrt(jnp.mean(x * x, axis=-1, keepdims=True) + NORM_EPS) * g


def _layernorm(x, g, b):
    xc = x - jnp.mean(x, axis=-1, keepdims=True)
    return xc * lax.rsqrt(jnp.mean(xc * xc, axis=-1, keepdims=True) + NORM_EPS) * g + b


def _gelu(x):
    return 0.5 * x * (1.0 + lax.erf(x * SQRT_HALF))


def _rope(x, cos, sin):
    half = ROT_DIM // 2
    dim = lax.broadcasted_iota(jnp.int32, (x.shape[0], LANES), 1) % HEAD_DIM
    second_half = (dim >= half) & (dim < ROT_DIM)
    out = []
    for j in range(x.shape[1] // LANES):
        xj = x[:, j * LANES:(j + 1) * LANES]
        partner = jnp.where(second_half, pltpu.roll(xj, half, axis=1), pltpu.roll(xj, LANES - half, axis=1))
        out.append(xj * cos + partner * sin)
    return jnp.concatenate(out, axis=1)


def _split_rope_tables(rope):
    tables = [rope[:, j * LANES:(j + 1) * LANES] for j in range(N_ROPE_TABLES)]
    return tables[:N_ROPE_TABLES // 2], tables[N_ROPE_TABLES // 2:]


def _head_lane_masks_bf16():
    lane = np.arange(KV_WIDTH)
    masks = np.stack([(lane // HEAD_DIM == h) for h in range(KV_HEADS)]).astype(np.float32)
    return jnp.asarray(np.broadcast_to(masks[:, None, :], (KV_HEADS, ATT_BLOCK, KV_WIDTH)), dtype=BF16)


def _masked_sgu_weights(sguw_ref, size):
    i = lax.broadcasted_iota(jnp.int32, (size, size), 0) // CHUNK
    j = lax.broadcasted_iota(jnp.int32, (size, size), 1) // CHUNK
    keep = i >= j
    return [jnp.where(keep, sguw_ref[g, :size, :size], 0.0).astype(BF16) for g in range(SGU_GROUPS)]


def _stack_heads(q_bf, lane_masks_bf):
    return jnp.concatenate([q_bf * lane_masks_bf[h] for h in range(KV_HEADS)], axis=0)


def _band_mask(s):
    half = ATT_KEYS // 2
    lane = lax.broadcasted_iota(jnp.int32, (CHUNK, half), 1)
    top = jnp.concatenate([s[:CHUNK, :half], jnp.where(lane < CHUNK, s[:CHUNK, half:], MASK_VALUE)], axis=1)
    bot = jnp.concatenate([jnp.where(lane >= CHUNK, s[CHUNK:, :half], MASK_VALUE), s[CHUNK:, half:]], axis=1)
    return jnp.concatenate([top, bot], axis=0)


def _attention_probs(q_stack, k_win, mask, sinks, rows):
    s = _dot_nt(q_stack, k_win)
    es, invs = [], []
    for h in range(KV_HEADS):
        sh = mask(s[h * rows:(h + 1) * rows])
        m = jnp.max(sh, axis=-1, keepdims=True)
        e = jnp.exp2(sh - m)
        den = jnp.sum(e, axis=-1, keepdims=True) + jnp.exp2(sinks[h] - m)
        es.append(e.astype(BF16))
        invs.append(1.0 / den)
    return jnp.concatenate(es, axis=0), invs


def _attention_output(e_stack, invs, v_win, rows):
    o = _dot(e_stack, v_win)
    lane = lax.broadcasted_iota(jnp.int32, (rows, KV_WIDTH), 1)
    out = o[(KV_HEADS - 1) * rows:] * invs[KV_HEADS - 1]
    for h in range(KV_HEADS - 2, -1, -1):
        out = jnp.where(lane < (h + 1) * HEAD_DIM, o[h * rows:(h + 1) * rows] * invs[h], out)
    return out


def _merge_and_residual(x, xn_ref, a_ref, b_ref, win_ref, wa_ref, wb_ref, wo_ref, gpost):
    ga = jax.nn.sigmoid(_dot(xn_ref[...], win_ref[:, OFF_GA:OFF_GA + D_MODEL]))
    m = ga * _dot(a_ref[...], wa_ref[...])
    gb = jax.nn.sigmoid(_dot(xn_ref[...], win_ref[:, OFF_GB:OFF_GB + D_MODEL]))
    m = m + gb * _dot(b_ref[...], wb_ref[...])
    out = _dot(m.astype(BF16), wo_ref[...])
    return x + _rmsnorm(out, gpost)


def _mixer_prompt_kernel(sinks_ref, x_ref, rope_ref, gpre_ref, gpost_ref, win_ref,
                         lng_ref, lnb_ref, sguw_ref, sgub_ref, qmask_ref, wa_ref, wb_ref, wo_ref,
                         h_ref, klast_ref, vlast_ref,
                         xn_ref, gv_ref, vn_ref, a_ref, q_ref, kext_ref, vext_ref, b_ref):
    T = x_ref.shape[0]
    t = pl.program_id(1)

    @pl.when(t == 0)
    def _():
        kext_ref[0:SWA_WINDOW] = jnp.zeros((SWA_WINDOW, KV_WIDTH), BF16)
        vext_ref[0:SWA_WINDOW] = jnp.zeros((SWA_WINDOW, KV_WIDTH), BF16)

    @pl.when(t > 0)
    def _():
        kext_ref[0:SWA_WINDOW] = kext_ref[T:T + SWA_WINDOW]
        vext_ref[0:SWA_WINDOW] = vext_ref[T:T + SWA_WINDOW]

    w_sgu = _masked_sgu_weights(sguw_ref, SGU_CHUNK)
    lane_masks_bf = [qmask_ref[h] for h in range(KV_HEADS)]
    q_chunk = lax.broadcasted_iota(jnp.int32, (ATT_BLOCK, ATT_KEYS), 0) // CHUNK
    key_idx = lax.broadcasted_iota(jnp.int32, (ATT_BLOCK, ATT_KEYS), 1)
    key_chunk = key_idx // CHUNK
    band = (key_chunk >= q_chunk) & (key_chunk <= q_chunk + SWA_WINDOW // CHUNK)
    has_prev = jnp.where(t > 0, SWA_WINDOW, 0)
    band_first = band & (key_idx + has_prev >= SWA_WINDOW)
    first_block_mask = lambda s: jnp.where(band_first, s, MASK_VALUE)

    def norm_in(r0):
        rows = slice(r0, r0 + MIX_SUB)
        xn_ref[rows] = _rmsnorm(x_ref[rows], gpre_ref[...]).astype(BF16)

    def gelu_v(r0):
        rows = slice(r0, r0 + MIX_SUB)
        for c in range(D_MODEL // MXU_COLS):
            cols = slice(c * MXU_COLS, (c + 1) * MXU_COLS)
            gv_ref[rows, cols] = _gelu(
                _dot(xn_ref[rows], win_ref[:, OFF_V + c * MXU_COLS:OFF_V + (c + 1) * MXU_COLS]))

    def attention_inputs(r0):
        rows = slice(r0, r0 + MIX_SUB)
        k_tables, q_tables = _split_rope_tables(rope_ref[rows])
        q_ref[rows] = _rope(_dot(xn_ref[rows], win_ref[:, OFF_Q:OFF_Q + Q_WIDTH]), *q_tables).astype(BF16)
        kva = _dot(xn_ref[rows], win_ref[:, OFF_K:OFF_VA + KV_WIDTH])
        k = _rope(kva[:, :KV_WIDTH], *k_tables)
        va = kva[:, KV_WIDTH:]
        if r0 + MIX_SUB == T:
            klast_ref[...] = k[MIX_SUB - SWA_WINDOW:]
            vlast_ref[...] = va[MIX_SUB - SWA_WINDOW:]
        kext_ref[SWA_WINDOW + r0:SWA_WINDOW + r0 + MIX_SUB] = k.astype(BF16)
        vext_ref[SWA_WINDOW + r0:SWA_WINDOW + r0 + MIX_SUB] = va.astype(BF16)

    def norm_v(r0):
        rows = slice(r0, r0 + MIX_SUB)
        vn_ref[rows] = _layernorm(gv_ref[rows], lng_ref[...], lnb_ref[...]).astype(BF16)

    def gating(r0):
        n_chunks = MIX_SUB // SGU_CHUNK
        groups_per_step = MXU_COLS // SGU_GROUP_DIM
        for c in range(D_MODEL // MXU_COLS):
            gu = _gelu(_dot(xn_ref[r0:r0 + MIX_SUB],
                            win_ref[:, OFF_U + c * MXU_COLS:OFF_U + (c + 1) * MXU_COLS]))
            for gg in range(groups_per_step):
                g = c * groups_per_step + gg
                gcols = slice(g * SGU_GROUP_DIM, (g + 1) * SGU_GROUP_DIM)
                rhs = jnp.concatenate(
                    [vn_ref[r0 + n * SGU_CHUNK:r0 + (n + 1) * SGU_CHUNK, gcols] for n in range(n_chunks)], axis=1)
                s = _dot(w_sgu[g], rhs)
                bias = sgub_ref[:, gcols]
                for n in range(n_chunks):
                    sn = s[:, n * SGU_GROUP_DIM:(n + 1) * SGU_GROUP_DIM] + bias
                    gun = gu[n * SGU_CHUNK:(n + 1) * SGU_CHUNK, gg * SGU_GROUP_DIM:(gg + 1) * SGU_GROUP_DIM]
                    a_ref[r0 + n * SGU_CHUNK:r0 + (n + 1) * SGU_CHUNK, gcols] = (gun * sn).astype(BF16)

    def attention(r0):
        blocks = [(i, g) for i in range(r0 // ATT_BLOCK, (r0 + MIX_SUB) // ATT_BLOCK) for g in range(Q_GROUP)]

        def probs(i, g):
            rows = slice(i * ATT_BLOCK, (i + 1) * ATT_BLOCK)
            krows = slice(i * ATT_BLOCK, i * ATT_BLOCK + ATT_KEYS)
            sinks = [sinks_ref[h * Q_GROUP + g] * LOG2_E for h in range(KV_HEADS)]
            q_stack = _stack_heads(q_ref[rows, g * KV_WIDTH:(g + 1) * KV_WIDTH], lane_masks_bf)
            mask = first_block_mask if i == 0 else _band_mask
            return _attention_probs(q_stack, kext_ref[krows, :], mask, sinks, ATT_BLOCK)

        def output(i, g, e_stack, invs):
            rows = slice(i * ATT_BLOCK, (i + 1) * ATT_BLOCK)
            krows = slice(i * ATT_BLOCK, i * ATT_BLOCK + ATT_KEYS)
            o = _attention_output(e_stack, invs, vext_ref[krows, :], ATT_BLOCK)
            b_ref[rows, g * KV_WIDTH:(g + 1) * KV_WIDTH] = o.astype(BF16)

        for block in blocks:
            output(*block, *probs(*block))

    def merge(r0):
        rows = slice(r0, r0 + MIX_SUB)
        ga = jax.nn.sigmoid(_dot(xn_ref[rows], win_ref[:, OFF_GA:OFF_GA + D_MODEL]))
        m = ga * _dot(a_ref[rows], wa_ref[...])
        gb = jax.nn.sigmoid(_dot(xn_ref[rows], win_ref[:, OFF_GB:OFF_GB + D_MODEL]))
        m = m + gb * _dot(b_ref[rows], wb_ref[...])
        out = _dot(m.astype(BF16), wo_ref[...])
        h_ref[rows] = x_ref[rows] + _rmsnorm(out, gpost_ref[...])

    for phase in (norm_in, gelu_v, norm_v, gating, attention_inputs, attention, merge):
        for r0 in range(0, T, MIX_SUB):
            phase(r0)


def _mixer_sample_kernel(sinks_ref, x_ref, rope_ref, gpre_ref, gpost_ref, win_ref,
                         lng_ref, lnb_ref, sguw_ref, sgub_ref, qmask_ref, wa_ref, wb_ref, wo_ref,
                         ck_ref, cv_ref,
                         h_ref, knew_ref, vnew_ref, vsgu_ref,
                         xn_ref, vn_ref, a_ref, q_ref, kn_ref, vnew_bf_ref, b_ref,
                         *, n_seq, seq_len):
    x = x_ref[...]
    xn_ref[...] = _rmsnorm(x, gpre_ref[...]).astype(BF16)

    vn = _layernorm(_gelu(_dot(xn_ref[...], win_ref[:, OFF_V:OFF_V + D_MODEL])), lng_ref[...], lnb_ref[...])
    vsgu_ref[...] = vn
    vn_ref[...] = vn.astype(BF16)
    gu = _gelu(_dot(xn_ref[...], win_ref[:, OFF_U:OFF_U + D_MODEL]))
    w_sgu = _masked_sgu_weights(sguw_ref, seq_len)
    for g in range(SGU_GROUPS):
        gcols = slice(g * SGU_GROUP_DIM, (g + 1) * SGU_GROUP_DIM)
        rhs = jnp.concatenate([vn_ref[n * seq_len:(n + 1) * seq_len, gcols] for n in range(n_seq)], axis=1)
        s = _dot(w_sgu[g], rhs)
        bias = sgub_ref[0:seq_len, gcols]
        for n in range(n_seq):
            rows = slice(n * seq_len, (n + 1) * seq_len)
            sn = s[:, n * SGU_GROUP_DIM:(n + 1) * SGU_GROUP_DIM] + bias
            a_ref[rows, gcols] = (gu[rows, gcols] * sn).astype(BF16)

    k_tables, q_tables = _split_rope_tables(rope_ref[...])
    q_ref[...] = _rope(_dot(xn_ref[...], win_ref[:, OFF_Q:OFF_Q + Q_WIDTH]), *q_tables).astype(BF16)
    k = _rope(_dot(xn_ref[...], win_ref[:, OFF_K:OFF_K + KV_WIDTH]), *k_tables)
    va = _dot(xn_ref[...], win_ref[:, OFF_VA:OFF_VA + KV_WIDTH])
    knew_ref[...] = k
    vnew_ref[...] = va
    kn_ref[...] = k.astype(BF16)
    vnew_bf_ref[...] = va.astype(BF16)

    n_keys = SWA_WINDOW + seq_len
    pad = jnp.zeros((ATT_KEYS - n_keys, KV_WIDTH), BF16)
    lane_masks_bf = [qmask_ref[h, 0:seq_len, :] for h in range(KV_HEADS)]
    real_key = lax.broadcasted_iota(jnp.int32, (seq_len, ATT_KEYS), 1) < n_keys
    mask = lambda s: jnp.where(real_key, s, MASK_VALUE)

    def per_sequence(n, carry):
        new_rows = pl.ds(pl.multiple_of(n * seq_len, seq_len), seq_len)
        old_rows = pl.ds(pl.multiple_of(n * SWA_WINDOW, SWA_WINDOW), SWA_WINDOW)
        k_win = jnp.concatenate([ck_ref[old_rows, :].astype(BF16), kn_ref[new_rows, :], pad], axis=0)
        v_win = jnp.concatenate([cv_ref[old_rows, :].astype(BF16), vnew_bf_ref[new_rows, :], pad], axis=0)
        for g in range(Q_GROUP):
            gcols = slice(g * KV_WIDTH, (g + 1) * KV_WIDTH)
            sinks = [sinks_ref[h * Q_GROUP + g] * LOG2_E for h in range(KV_HEADS)]
            e_stack, invs = _attention_probs(_stack_heads(q_ref[new_rows, gcols], lane_masks_bf), k_win, mask,
                                             sinks, seq_len)
            b_ref[new_rows, gcols] = _attention_output(e_stack, invs, v_win, seq_len).astype(BF16)
        return carry

    lax.fori_loop(0, n_seq, per_sequence, 0, unroll=SAMPLE_UNROLL)

    h_ref[...] = _merge_and_residual(x, xn_ref, a_ref, b_ref, win_ref, wa_ref, wb_ref, wo_ref, gpost_ref[...])


def _ffn_kernel(h_ref, gpre_ref, gpost_ref, w1_ref, w2_ref, y_ref, hn_ref):
    n_sub = max(1, h_ref.shape[0] // FFN_SUB)
    sub = h_ref.shape[0] // n_sub
    for r in range(n_sub):
        rows = slice(r * sub, (r + 1) * sub)
        h = h_ref[rows]
        hn_ref[rows] = _rmsnorm(h, gpre_ref[...]).astype(BF16)
        z = None
        for j in range(D_FF // FF_CHUNK):
            f = _dot(hn_ref[rows], w1_ref[:, j * FF_CHUNK:(j + 1) * FF_CHUNK])
            f = jnp.square(jnp.maximum(f, 0.0)).astype(BF16)
            zj = _dot(f, w2_ref[j * FF_CHUNK:(j + 1) * FF_CHUNK, :])
            z = zj if z is None else z + zj
        y_ref[rows] = h + _rmsnorm(z, gpost_ref[...])


def _resident(shape):
    return pl.BlockSpec(shape, lambda *_: (0,) * len(shape), pipeline_mode=pl.Buffered(1))


def _rope_tables(pos):
    half = ROT_DIM // 2
    inv = ROPE_THETA ** (-jnp.arange(half, dtype=F32) * 2.0 / ROT_DIM)
    ang = pos[:, None] * inv[None, :]
    cos, sin = jnp.cos(ang), jnp.sin(ang)
    n = pos.shape[0]
    cos_t = jnp.concatenate([cos, cos, jnp.ones((n, HEAD_DIM - ROT_DIM), F32)], axis=1)
    sin_t = jnp.concatenate([-sin, sin, jnp.zeros((n, HEAD_DIM - ROT_DIM), F32)], axis=1)
    tile = lambda t: jnp.tile(t, (1, LANES // HEAD_DIM))
    k_tables = [tile(cos_t), tile(sin_t)]
    return jnp.concatenate(k_tables + [t * SCORE_SCALE for t in k_tables], axis=1)


def _q_heads_by_group(w, axis):
    shape = w.shape
    split = shape[:axis] + (KV_HEADS, Q_GROUP, HEAD_DIM) + shape[axis + 1:]
    return jnp.swapaxes(w.reshape(split), axis, axis + 1).reshape(shape)


def _shared_mixer_operands(w):
    specs = [
        _resident((1, D_MODEL)), _resident((1, D_MODEL)), _resident((D_MODEL, IN_WIDTH)),
        _resident((1, D_MODEL)), _resident((1, D_MODEL)),
        _resident((SGU_GROUPS, SGU_CHUNK, SGU_CHUNK)), _resident((SGU_CHUNK, D_MODEL)),
        _resident((KV_HEADS, ATT_BLOCK, KV_WIDTH)),
        _resident((D_MODEL, D_MODEL)), _resident((Q_WIDTH, D_MODEL)), _resident((D_MODEL, D_MODEL)),
    ]
    args = [w["g_mix_pre"], w["g_mix_post"], w["w_in"], w["sgu_ln_g"], w["sgu_ln_b"], w["sgu_w"],
            w["sgu_b_rows"], _head_lane_masks_bf16(), w["w_branch_a"], w["w_branch_b"], w["w_out"]]
    return specs, args


def _mixer_prompt(x, w):
    B, S, _ = x.shape
    T = MIX_TILE
    rope = _rope_tables(jnp.arange(S, dtype=F32))
    table_spec = pl.BlockSpec((T, N_ROPE_TABLES * LANES), lambda b, t: (t, 0))
    shared_specs, shared_args = _shared_mixer_operands(w)
    last_spec = pl.BlockSpec((None, SWA_WINDOW, KV_WIDTH), lambda b, t: (b, 0, 0))
    return pl.pallas_call(
        _mixer_prompt_kernel,
        grid=(B, S // T),
        in_specs=[pl.BlockSpec(memory_space=pltpu.SMEM),
                  pl.BlockSpec((None, T, D_MODEL), lambda b, t: (b, t, 0)),
                  table_spec] + shared_specs,
        out_specs=[pl.BlockSpec((None, T, D_MODEL), lambda b, t: (b, t, 0)), last_spec, last_spec],
        out_shape=[jax.ShapeDtypeStruct((B, S, D_MODEL), F32),
                   jax.ShapeDtypeStruct((B, SWA_WINDOW, KV_WIDTH), F32),
                   jax.ShapeDtypeStruct((B, SWA_WINDOW, KV_WIDTH), F32)],
        scratch_shapes=[
            pltpu.VMEM((T, D_MODEL), BF16),
            pltpu.VMEM((T, D_MODEL), F32),
            pltpu.VMEM((T, D_MODEL), BF16),
            pltpu.VMEM((T, D_MODEL), BF16),
            pltpu.VMEM((T, Q_WIDTH), BF16),
            pltpu.VMEM((T + SWA_WINDOW, KV_WIDTH), BF16),
            pltpu.VMEM((T + SWA_WINDOW, KV_WIDTH), BF16),
            pltpu.VMEM((T, Q_WIDTH), BF16),
        ],
        compiler_params=pltpu.CompilerParams(
            dimension_semantics=("arbitrary", "arbitrary"), vmem_limit_bytes=VMEM_LIMIT_BYTES),
        name="mixer_prompt",
    )(w["attn_sinks"], x, rope, *shared_args)


def _mixer_sample(x, cache_k, cache_v, w):
    n_seq, seq_len, _ = x.shape
    rows = n_seq * seq_len
    rope = jnp.tile(_rope_tables(PAST_LEN + jnp.arange(seq_len, dtype=F32)), (n_seq, 1))
    shared_specs, shared_args = _shared_mixer_operands(w)
    cache_rows = n_seq * SWA_WINDOW
    return pl.pallas_call(
        functools.partial(_mixer_sample_kernel, n_seq=n_seq, seq_len=seq_len),
        grid=(1,),
        in_specs=[pl.BlockSpec(memory_space=pltpu.SMEM), _resident((rows, D_MODEL)),
                  _resident((rows, N_ROPE_TABLES * LANES))]
                 + shared_specs + [_resident((cache_rows, KV_WIDTH)), _resident((cache_rows, KV_WIDTH))],
        out_specs=[_resident((rows, D_MODEL)), _resident((rows, KV_WIDTH)), _resident((rows, KV_WIDTH)),
                   _resident((rows, D_MODEL))],
        out_shape=[jax.ShapeDtypeStruct((rows, D_MODEL), F32),
                   jax.ShapeDtypeStruct((rows, KV_WIDTH), F32),
                   jax.ShapeDtypeStruct((rows, KV_WIDTH), F32),
                   jax.ShapeDtypeStruct((rows, D_MODEL), F32)],
        scratch_shapes=[
            pltpu.VMEM((rows, D_MODEL), BF16),
            pltpu.VMEM((rows, D_MODEL), BF16),
            pltpu.VMEM((rows, D_MODEL), BF16),
            pltpu.VMEM((rows, Q_WIDTH), BF16),
            pltpu.VMEM((rows, KV_WIDTH), BF16),
            pltpu.VMEM((rows, KV_WIDTH), BF16),
            pltpu.VMEM((rows, Q_WIDTH), BF16),
        ],
        compiler_params=pltpu.CompilerParams(
            dimension_semantics=("arbitrary",), vmem_limit_bytes=VMEM_LIMIT_BYTES),
        name="mixer_sample",
    )(w["attn_sinks"], x.reshape(rows, D_MODEL), rope, *shared_args,
      cache_k.reshape(cache_rows, KV_WIDTH), cache_v.reshape(cache_rows, KV_WIDTH))


def _ffn(h, w, tile, name):
    rows = h.shape[0]
    return pl.pallas_call(
        _ffn_kernel,
        grid=(rows // tile,),
        in_specs=[pl.BlockSpec((tile, D_MODEL), lambda i: (i, 0)),
                  _resident((1, D_MODEL)), _resident((1, D_MODEL)),
                  _resident((D_MODEL, D_FF)), _resident((D_FF, D_MODEL))],
        out_specs=pl.BlockSpec((tile, D_MODEL), lambda i: (i, 0)),
        out_shape=jax.ShapeDtypeStruct((rows, D_MODEL), F32),
        scratch_shapes=[pltpu.VMEM((tile, D_MODEL), BF16)],
        compiler_params=pltpu.CompilerParams(
            dimension_semantics=("arbitrary",), vmem_limit_bytes=VMEM_LIMIT_BYTES),
        name=name,
    )(h, w["g_ffn_pre"], w["g_ffn_post"], w["w_ff1"], w["w_ff2"])


def _layer(h_p, h_s, ck, cv, w):
    B, S, _ = h_p.shape
    n_seq, seq_len, _ = h_s.shape
    h_p, k_last, v_last = _mixer_prompt(h_p, w)
    y_p = _ffn(h_p.reshape(B * S, D_MODEL), w, FFN_TILE, "ffn_prompt").reshape(B, S, D_MODEL)
    h_s, k_new, v_new, v_sgu = _mixer_sample(h_s, ck, cv, w)
    y_s = _ffn(h_s, w, n_seq * seq_len, "ffn_sample").reshape(n_seq, seq_len, D_MODEL)
    return (y_p, y_s,
            k_last.reshape(B, SWA_WINDOW, KV_HEADS, HEAD_DIM), v_last.reshape(B, SWA_WINDOW, KV_HEADS, HEAD_DIM),
            k_new.reshape(n_seq, seq_len, KV_HEADS, HEAD_DIM), v_new.reshape(n_seq, seq_len, KV_HEADS, HEAD_DIM),
            v_sgu.reshape(n_seq, seq_len, D_MODEL))


def kernel(x_prompt, x_sample, cache_swa_k, cache_swa_v, w_in, sgu_ln_g, sgu_ln_b, sgu_w, sgu_b, attn_sinks,
           w_branch_a, w_branch_b, w_out, g_mix_pre, g_mix_post, g_ffn_pre, g_ffn_post, w_ff1, w_ff2):
    depth = w_in.shape[0]
    h_p, h_s = x_prompt, x_sample
    per_layer = []
    for l in range(depth):
        row = lambda v: v[l].reshape(1, -1)
        w_in_l = w_in[l].astype(BF16)
        w_in_l = jnp.concatenate([w_in_l[:, :OFF_Q], _q_heads_by_group(w_in_l[:, OFF_Q:OFF_K], 1),
                                  w_in_l[:, OFF_K:]], axis=1)
        w = {
            "w_in": w_in_l, "w_branch_a": w_branch_a[l].astype(BF16),
            "w_branch_b": _q_heads_by_group(w_branch_b[l].astype(BF16), 0), "w_out": w_out[l].astype(BF16),
            "w_ff1": w_ff1[l].astype(BF16), "w_ff2": w_ff2[l].astype(BF16),
            "sgu_ln_g": row(sgu_ln_g), "sgu_ln_b": row(sgu_ln_b), "sgu_w": sgu_w[l],
            "sgu_b_rows": jnp.repeat(sgu_b[l].T, SGU_GROUP_DIM, axis=1),
            "attn_sinks": attn_sinks[l],
            "g_mix_pre": row(g_mix_pre), "g_mix_post": row(g_mix_post),
            "g_ffn_pre": row(g_ffn_pre), "g_ffn_post": row(g_ffn_post),
        }
        h_p, h_s, *states = _layer(h_p, h_s, cache_swa_k[l], cache_swa_v[l], w)
        per_layer.append(states)
    stacked = [jnp.stack([states[i] for states in per_layer]) for i in range(5)]
    return (h_p, h_s, *stacked)
```

```python
import functools

import numpy as np
import jax
import jax.numpy as jnp
from jax import lax
from jax.experimental import pallas as pl
from jax.experimental.pallas import tpu as pltpu

D_MODEL = 1024
CHUNK = 64
SGU_CHUNK = 128
SGU_GROUPS = 8
SGU_GROUP_DIM = D_MODEL // SGU_GROUPS
N_HEADS = 16
KV_HEADS = 4
HEAD_DIM = 64
Q_GROUP = N_HEADS // KV_HEADS
SWA_WINDOW = 128
ROT_DIM = HEAD_DIM // 4
ROPE_THETA = 500000.0
D_FF = 4 * D_MODEL
NORM_EPS = 1e-6
PAST_LEN = 4096
MASK_VALUE = -1e30

Q_WIDTH = N_HEADS * HEAD_DIM
KV_WIDTH = KV_HEADS * HEAD_DIM
OFF_U = 0
OFF_V = OFF_U + D_MODEL
OFF_Q = OFF_V + D_MODEL
OFF_K = OFF_Q + Q_WIDTH
OFF_VA = OFF_K + KV_WIDTH
OFF_GA = OFF_VA + KV_WIDTH
OFF_GB = OFF_GA + D_MODEL
IN_WIDTH = OFF_GB + D_MODEL

LANES = 128
MXU_COLS = 256
VMEM_LIMIT_BYTES = 52 * 1024 * 1024
MIX_TILE = 512
MIX_SUB = MIX_TILE
FFN_TILE = 1024
FFN_SUB = 512
ATT_BLOCK = 2 * CHUNK
ATT_KEYS = ATT_BLOCK + SWA_WINDOW
FF_CHUNK = 2048
SAMPLE_UNROLL = 4

F32 = jnp.float32
BF16 = jnp.bfloat16
SQRT_HALF = np.sqrt(0.5).astype(np.float32)
LOG2_E = np.float32(np.log2(np.e))
SCORE_SCALE = np.float32(HEAD_DIM ** -0.5) * LOG2_E
N_ROPE_TABLES = 4


def _dot(a, b):
    return jnp.dot(a, b, preferred_element_type=F32)


def _dot_nt(a, b):
    return lax.dot_general(a, b, (((1,), (1,)), ((), ())), preferred_element_type=F32)


def _rmsnorm(x, g):
    return x * lax.rsqrt(jnp.mean(x * x, axis=-1, keepdims=True) + NORM_EPS) * g


def _layernorm(x, g, b):
    xc = x - jnp.mean(x, axis=-1, keepdims=True)
    return xc * lax.rsqrt(jnp.mean(xc * xc, axis=-1, keepdims=True) + NORM_EPS) * g + b


def _gelu(x):
    return 0.5 * x * (1.0 + lax.erf(x * SQRT_HALF))


def _rope(x, cos, sin):
    half = ROT_DIM // 2
    dim = lax.broadcasted_iota(jnp.int32, (x.shape[0], LANES), 1) % HEAD_DIM
    second_half = (dim >= half) & (dim < ROT_DIM)
    out = []
    for j in range(x.shape[1] // LANES):
        xj = x[:, j * LANES:(j + 1) * LANES]
        partner = jnp.where(second_half, pltpu.roll(xj, half, axis=1), pltpu.roll(xj, LANES - half, axis=1))
        out.append(xj * cos + partner * sin)
    return jnp.concatenate(out, axis=1)


def _split_rope_tables(rope):
    tables = [rope[:, j * LANES:(j + 1) * LANES] for j in range(N_ROPE_TABLES)]
    return tables[:N_ROPE_TABLES // 2], tables[N_ROPE_TABLES // 2:]


def _head_lane_masks_bf16():
    lane = np.arange(KV_WIDTH)
    masks = np.stack([(lane // HEAD_DIM == h) for h in range(KV_HEADS)]).astype(np.float32)
    return jnp.asarray(np.broadcast_to(masks[:, None, :], (KV_HEADS, ATT_BLOCK, KV_WIDTH)), dtype=BF16)


def _masked_sgu_weights(sguw_ref, size):
    i = lax.broadcasted_iota(jnp.int32, (size, size), 0) // CHUNK
    j = lax.broadcasted_iota(jnp.int32, (size, size), 1) // CHUNK
    keep = i >= j
    return [jnp.where(keep, sguw_ref[g, :size, :size], 0.0).astype(BF16) for g in range(SGU_GROUPS)]


def _stack_heads(q_bf, lane_masks_bf):
    return jnp.concatenate([q_bf * lane_masks_bf[h] for h in range(KV_HEADS)], axis=0)


def _band_mask(s):
    half = ATT_KEYS // 2
    lane = lax.broadcasted_iota(jnp.int32, (CHUNK, half), 1)
    top = jnp.concatenate([s[:CHUNK, :half], jnp.where(lane < CHUNK, s[:CHUNK, half:], MASK_VALUE)], axis=1)
    bot = jnp.concatenate([jnp.where(lane >= CHUNK, s[CHUNK:, :half], MASK_VALUE), s[CHUNK:, half:]], axis=1)
    return jnp.concatenate([top, bot], axis=0)


def _attention_probs(q_stack, k_win, mask, sinks, rows):
    s = _dot_nt(q_stack, k_win)
    es, invs = [], []
    for n, sink in enumerate(sinks):
        sh = mask(s[n * rows:(n + 1) * rows])
        m = jnp.maximum(jnp.max(sh, axis=-1, keepdims=True), sink)
        e = jnp.exp2(sh - m)
        den = jnp.sum(e, axis=-1, keepdims=True) + jnp.exp2(sink - m)
        es.append(e.astype(BF16))
        invs.append(1.0 / den)
    return jnp.concatenate(es, axis=0), invs


def _merge_heads(o, invs, rows):
    lane = lax.broadcasted_iota(jnp.int32, (rows, KV_WIDTH), 1)
    out = o[(KV_HEADS - 1) * rows:] * invs[KV_HEADS - 1]
    for h in range(KV_HEADS - 2, -1, -1):
        out = jnp.where(lane < (h + 1) * HEAD_DIM, o[h * rows:(h + 1) * rows] * invs[h], out)
    return out


def _attention_output(e_stack, invs, v_win, rows):
    return _merge_heads(_dot(e_stack, v_win), invs, rows)


def _merge_and_residual(x, xn_ref, a_ref, b_ref, win_ref, wa_ref, wb_ref, wo_ref, gpost):
    ga = jax.nn.sigmoid(_dot(xn_ref[...], win_ref[:, OFF_GA:OFF_GA + D_MODEL]))
    m = ga * _dot(a_ref[...], wa_ref[...])
    gb = jax.nn.sigmoid(_dot(xn_ref[...], win_ref[:, OFF_GB:OFF_GB + D_MODEL]))
    m = m + gb * _dot(b_ref[...], wb_ref[...])
    out = _dot(m.astype(BF16), wo_ref[...])
    return x + _rmsnorm(out, gpost)


def _mixer_prompt_kernel(sinks_ref, x_ref, rope_ref, gpre_ref, gpost_ref, win_ref,
                         lng_ref, lnb_ref, sguw_ref, sgub_ref, qmask_ref, wa_ref, wb_ref, wo_ref,
                         h_ref, klast_ref, vlast_ref,
                         xn_ref, gv_ref, vn_ref, a_ref, q_ref, kext_ref, vext_ref, b_ref):
    T = x_ref.shape[0]
    t = pl.program_id(1)

    @pl.when(t == 0)
    def _():
        kext_ref[0:SWA_WINDOW] = jnp.zeros((SWA_WINDOW, KV_WIDTH), BF16)
        vext_ref[0:SWA_WINDOW] = jnp.zeros((SWA_WINDOW, KV_WIDTH), BF16)

    @pl.when(t > 0)
    def _():
        kext_ref[0:SWA_WINDOW] = kext_ref[T:T + SWA_WINDOW]
        vext_ref[0:SWA_WINDOW] = vext_ref[T:T + SWA_WINDOW]

    w_sgu = _masked_sgu_weights(sguw_ref, SGU_CHUNK)
    lane_masks_bf = [qmask_ref[h] for h in range(KV_HEADS)]
    q_chunk = lax.broadcasted_iota(jnp.int32, (ATT_BLOCK, ATT_KEYS), 0) // CHUNK
    key_idx = lax.broadcasted_iota(jnp.int32, (ATT_BLOCK, ATT_KEYS), 1)
    key_chunk = key_idx // CHUNK
    band = (key_chunk >= q_chunk) & (key_chunk <= q_chunk + SWA_WINDOW // CHUNK)
    has_prev = jnp.where(t > 0, SWA_WINDOW, 0)
    band_first = band & (key_idx + has_prev >= SWA_WINDOW)
    first_block_mask = lambda s: jnp.where(band_first, s, MASK_VALUE)

    def norm_in(r0):
        rows = slice(r0, r0 + MIX_SUB)
        xn_ref[rows] = _rmsnorm(x_ref[rows], gpre_ref[...]).astype(BF16)

    def gelu_v(r0):
        rows = slice(r0, r0 + MIX_SUB)
        for c in range(D_MODEL // MXU_COLS):
            cols = slice(c * MXU_COLS, (c + 1) * MXU_COLS)
            gv_ref[rows, cols] = _gelu(
                _dot(xn_ref[rows], win_ref[:, OFF_V + c * MXU_COLS:OFF_V + (c + 1) * MXU_COLS]))

    def attention_inputs(r0):
        rows = slice(r0, r0 + MIX_SUB)
        k_tables, q_tables = _split_rope_tables(rope_ref[rows])
        q_ref[rows] = _rope(_dot(xn_ref[rows], win_ref[:, OFF_Q:OFF_Q + Q_WIDTH]), *q_tables).astype(BF16)
        kva = _dot(xn_ref[rows], win_ref[:, OFF_K:OFF_VA + KV_WIDTH])
        k = _rope(kva[:, :KV_WIDTH], *k_tables)
        va = kva[:, KV_WIDTH:]
        if r0 + MIX_SUB == T:
            klast_ref[...] = k[MIX_SUB - SWA_WINDOW:]
            vlast_ref[...] = va[MIX_SUB - SWA_WINDOW:]
        kext_ref[SWA_WINDOW + r0:SWA_WINDOW + r0 + MIX_SUB] = k.astype(BF16)
        vext_ref[SWA_WINDOW + r0:SWA_WINDOW + r0 + MIX_SUB] = va.astype(BF16)

    def norm_v(r0):
        rows = slice(r0, r0 + MIX_SUB)
        vn_ref[rows] = _layernorm(gv_ref[rows], lng_ref[...], lnb_ref[...]).astype(BF16)

    def gating(r0):
        n_chunks = MIX_SUB // SGU_CHUNK
        groups_per_step = MXU_COLS // SGU_GROUP_DIM
        for c in range(D_MODEL // MXU_COLS):
            gu = _gelu(_dot(xn_ref[r0:r0 + MIX_SUB],
                            win_ref[:, OFF_U + c * MXU_COLS:OFF_U + (c + 1) * MXU_COLS]))
            for gg in range(groups_per_step):
                g = c * groups_per_step + gg
                gcols = slice(g * SGU_GROUP_DIM, (g + 1) * SGU_GROUP_DIM)
                rhs = jnp.concatenate(
                    [vn_ref[r0 + n * SGU_CHUNK:r0 + (n + 1) * SGU_CHUNK, gcols] for n in range(n_chunks)], axis=1)
                s = _dot(w_sgu[g], rhs)
                bias = sgub_ref[:, gcols]
                for n in range(n_chunks):
                    sn = s[:, n * SGU_GROUP_DIM:(n + 1) * SGU_GROUP_DIM] + bias
                    gun = gu[n * SGU_CHUNK:(n + 1) * SGU_CHUNK, gg * SGU_GROUP_DIM:(gg + 1) * SGU_GROUP_DIM]
                    a_ref[r0 + n * SGU_CHUNK:r0 + (n + 1) * SGU_CHUNK, gcols] = (gun * sn).astype(BF16)

    def attention(r0):
        blocks = [(i, g) for i in range(r0 // ATT_BLOCK, (r0 + MIX_SUB) // ATT_BLOCK) for g in range(Q_GROUP)]

        def probs(i, g):
            rows = slice(i * ATT_BLOCK, (i + 1) * ATT_BLOCK)
            krows = slice(i * ATT_BLOCK, i * ATT_BLOCK + ATT_KEYS)
            sinks = [sinks_ref[h * Q_GROUP + g] * LOG2_E for h in range(KV_HEADS)]
            q_stack = _stack_heads(q_ref[rows, g * KV_WIDTH:(g + 1) * KV_WIDTH], lane_masks_bf)
            mask = first_block_mask if i == 0 else _band_mask
            return _attention_probs(q_stack, kext_ref[krows, :], mask, sinks, ATT_BLOCK)

        def output(i, g, e_stack, invs):
            rows = slice(i * ATT_BLOCK, (i + 1) * ATT_BLOCK)
            krows = slice(i * ATT_BLOCK, i * ATT_BLOCK + ATT_KEYS)
            o = _attention_output(e_stack, invs, vext_ref[krows, :], ATT_BLOCK)
            b_ref[rows, g * KV_WIDTH:(g + 1) * KV_WIDTH] = o.astype(BF16)

        for block in blocks:
            output(*block, *probs(*block))

    def merge(r0):
        rows = slice(r0, r0 + MIX_SUB)
        ga = jax.nn.sigmoid(_dot(xn_ref[rows], win_ref[:, OFF_GA:OFF_GA + D_MODEL]))
        m = ga * _dot(a_ref[rows], wa_ref[...])
        gb = jax.nn.sigmoid(_dot(xn_ref[rows], win_ref[:, OFF_GB:OFF_GB + D_MODEL]))
        m = m + gb * _dot(b_ref[rows], wb_ref[...])
        out = _dot(m.astype(BF16), wo_ref[...])
        h_ref[rows] = x_ref[rows] + _rmsnorm(out, gpost_ref[...])

    for phase in (norm_in, gelu_v, norm_v, gating, attention_inputs, attention, merge):
        for r0 in range(0, T, MIX_SUB):
            phase(r0)


def _mixer_sample_kernel(sinks_ref, x_ref, rope_ref, gpre_ref, gpost_ref, win_ref,
                         lng_ref, lnb_ref, sguw_ref, sgub_ref, qmask_ref, wa_ref, wb_ref, wo_ref,
                         ck_ref, cv_ref,
                         h_ref, knew_ref, vnew_ref, vsgu_ref,
                         xn_ref, vn_ref, a_ref, q_ref, kn_ref, vnew_bf_ref, b_ref,
                         *, n_seq, seq_len):
    x = x_ref[...]
    xn_ref[...] = _rmsnorm(x, gpre_ref[...]).astype(BF16)

    vn = _layernorm(_gelu(_dot(xn_ref[...], win_ref[:, OFF_V:OFF_V + D_MODEL])), lng_ref[...], lnb_ref[...])
    vsgu_ref[...] = vn
    vn_ref[...] = vn.astype(BF16)
    gu = _gelu(_dot(xn_ref[...], win_ref[:, OFF_U:OFF_U + D_MODEL]))
    w_sgu = _masked_sgu_weights(sguw_ref, seq_len)
    for g in range(SGU_GROUPS):
        gcols = slice(g * SGU_GROUP_DIM, (g + 1) * SGU_GROUP_DIM)
        rhs = jnp.concatenate([vn_ref[n * seq_len:(n + 1) * seq_len, gcols] for n in range(n_seq)], axis=1)
        s = _dot(w_sgu[g], rhs)
        bias = sgub_ref[0:seq_len, gcols]
        for n in range(n_seq):
            rows = slice(n * seq_len, (n + 1) * seq_len)
            sn = s[:, n * SGU_GROUP_DIM:(n + 1) * SGU_GROUP_DIM] + bias
            a_ref[rows, gcols] = (gu[rows, gcols] * sn).astype(BF16)

    k_tables, q_tables = _split_rope_tables(rope_ref[...])
    q_ref[...] = _rope(_dot(xn_ref[...], win_ref[:, OFF_Q:OFF_Q + Q_WIDTH]), *q_tables).astype(BF16)
    k = _rope(_dot(xn_ref[...], win_ref[:, OFF_K:OFF_K + KV_WIDTH]), *k_tables)
    va = _dot(xn_ref[...], win_ref[:, OFF_VA:OFF_VA + KV_WIDTH])
    knew_ref[...] = k
    vnew_ref[...] = va
    kn_ref[...] = k.astype(BF16)
    vnew_bf_ref[...] = va.astype(BF16)

    n_keys = SWA_WINDOW + seq_len
    pad = jnp.zeros((ATT_KEYS - n_keys, KV_WIDTH), BF16)
    lane_masks_bf = [qmask_ref[h, 0:seq_len, :] for h in range(KV_HEADS)]
    real_key = lax.broadcasted_iota(jnp.int32, (seq_len, ATT_KEYS), 1) < n_keys
    mask = lambda s: jnp.where(real_key, s, MASK_VALUE)

    def per_sequence(n, carry):
        new_rows = pl.ds(pl.multiple_of(n * seq_len, seq_len), seq_len)
        old_rows = pl.ds(pl.multiple_of(n * SWA_WINDOW, SWA_WINDOW), SWA_WINDOW)
        k_win = jnp.concatenate([ck_ref[old_rows, :].astype(BF16), kn_ref[new_rows, :], pad], axis=0)
        v_win = jnp.concatenate([cv_ref[old_rows, :].astype(BF16), vnew_bf_ref[new_rows, :], pad], axis=0)
        q_stack = jnp.concatenate(
            [_stack_heads(q_ref[new_rows, g * KV_WIDTH:(g + 1) * KV_WIDTH], lane_masks_bf) for g in range(Q_GROUP)],
            axis=0)
        sinks = [sinks_ref[h * Q_GROUP + g] * LOG2_E for g in range(Q_GROUP) for h in range(KV_HEADS)]
        e_stack, invs = _attention_probs(q_stack, k_win, mask, sinks, seq_len)
        o = _dot(e_stack, v_win)
        group_rows = KV_HEADS * seq_len
        for g in range(Q_GROUP):
            b_ref[new_rows, g * KV_WIDTH:(g + 1) * KV_WIDTH] = _merge_heads(
                o[g * group_rows:(g + 1) * group_rows], invs[g * KV_HEADS:(g + 1) * KV_HEADS], seq_len).astype(BF16)
        return carry

    lax.fori_loop(0, n_seq, per_sequence, 0, unroll=SAMPLE_UNROLL)

    h_ref[...] = _merge_and_residual(x, xn_ref, a_ref, b_ref, win_ref, wa_ref, wb_ref, wo_ref, gpost_ref[...])


def _ffn_kernel(h_ref, gpre_ref, gpost_ref, w1_ref, w2_ref, y_ref, hn_ref):
    n_sub = max(1, h_ref.shape[0] // FFN_SUB)
    sub = h_ref.shape[0] // n_sub
    for r in range(n_sub):
        rows = slice(r * sub, (r + 1) * sub)
        h = h_ref[rows]
        hn_ref[rows] = _rmsnorm(h, gpre_ref[...]).astype(BF16)
        z = None
        for j in range(D_FF // FF_CHUNK):
            f = _dot(hn_ref[rows], w1_ref[:, j * FF_CHUNK:(j + 1) * FF_CHUNK])
            f = jnp.square(jnp.maximum(f, 0.0)).astype(BF16)
            zj = _dot(f, w2_ref[j * FF_CHUNK:(j + 1) * FF_CHUNK, :])
            z = zj if z is None else z + zj
        y_ref[rows] = h + _rmsnorm(z, gpost_ref[...])


def _resident(shape):
    return pl.BlockSpec(shape, lambda *_: (0,) * len(shape), pipeline_mode=pl.Buffered(1))


def _rope_tables(pos):
    half = ROT_DIM // 2
    inv = ROPE_THETA ** (-jnp.arange(half, dtype=F32) * 2.0 / ROT_DIM)
    ang = pos[:, None] * inv[None, :]
    cos, sin = jnp.cos(ang), jnp.sin(ang)
    n = pos.shape[0]
    cos_t = jnp.concatenate([cos, cos, jnp.ones((n, HEAD_DIM - ROT_DIM), F32)], axis=1)
    sin_t = jnp.concatenate([-sin, sin, jnp.zeros((n, HEAD_DIM - ROT_DIM), F32)], axis=1)
    tile = lambda t: jnp.tile(t, (1, LANES // HEAD_DIM))
    k_tables = [tile(cos_t), tile(sin_t)]
    return jnp.concatenate(k_tables + [t * SCORE_SCALE for t in k_tables], axis=1)


def _q_heads_by_group(w, axis):
    shape = w.shape
    split = shape[:axis] + (KV_HEADS, Q_GROUP, HEAD_DIM) + shape[axis + 1:]
    return jnp.swapaxes(w.reshape(split), axis, axis + 1).reshape(shape)


def _shared_mixer_operands(w):
    specs = [
        _resident((1, D_MODEL)), _resident((1, D_MODEL)), _resident((D_MODEL, IN_WIDTH)),
        _resident((1, D_MODEL)), _resident((1, D_MODEL)),
        _resident((SGU_GROUPS, SGU_CHUNK, SGU_CHUNK)), _resident((SGU_CHUNK, D_MODEL)),
        _resident((KV_HEADS, ATT_BLOCK, KV_WIDTH)),
        _resident((D_MODEL, D_MODEL)), _resident((Q_WIDTH, D_MODEL)), _resident((D_MODEL, D_MODEL)),
    ]
    args = [w["g_mix_pre"], w["g_mix_post"], w["w_in"], w["sgu_ln_g"], w["sgu_ln_b"], w["sgu_w"],
            w["sgu_b_rows"], _head_lane_masks_bf16(), w["w_branch_a"], w["w_branch_b"], w["w_out"]]
    return specs, args


def _mixer_prompt(x, w):
    B, S, _ = x.shape
    T = MIX_TILE
    rope = _rope_tables(jnp.arange(S, dtype=F32))
    table_spec = pl.BlockSpec((T, N_ROPE_TABLES * LANES), lambda b, t: (t, 0))
    shared_specs, shared_args = _shared_mixer_operands(w)
    last_spec = pl.BlockSpec((None, SWA_WINDOW, KV_WIDTH), lambda b, t: (b, 0, 0))
    return pl.pallas_call(
        _mixer_prompt_kernel,
        grid=(B, S // T),
        in_specs=[pl.BlockSpec(memory_space=pltpu.SMEM),
                  pl.BlockSpec((None, T, D_MODEL), lambda b, t: (b, t, 0)),
                  table_spec] + shared_specs,
        out_specs=[pl.BlockSpec((None, T, D_MODEL), lambda b, t: (b, t, 0)), last_spec, last_spec],
        out_shape=[jax.ShapeDtypeStruct((B, S, D_MODEL), F32),
                   jax.ShapeDtypeStruct((B, SWA_WINDOW, KV_WIDTH), F32),
                   jax.ShapeDtypeStruct((B, SWA_WINDOW, KV_WIDTH), F32)],
        scratch_shapes=[
            pltpu.VMEM((T, D_MODEL), BF16),
            pltpu.VMEM((T, D_MODEL), F32),
            pltpu.VMEM((T, D_MODEL), BF16),
            pltpu.VMEM((T, D_MODEL), BF16),
            pltpu.VMEM((T, Q_WIDTH), BF16),
            pltpu.VMEM((T + SWA_WINDOW, KV_WIDTH), BF16),
            pltpu.VMEM((T + SWA_WINDOW, KV_WIDTH), BF16),
            pltpu.VMEM((T, Q_WIDTH), BF16),
        ],
        compiler_params=pltpu.CompilerParams(
            dimension_semantics=("arbitrary", "arbitrary"), vmem_limit_bytes=VMEM_LIMIT_BYTES),
        name="mixer_prompt",
    )(w["attn_sinks"], x, rope, *shared_args)


def _mixer_sample(x, cache_k, cache_v, w):
    n_seq, seq_len, _ = x.shape
    rows = n_seq * seq_len
    rope = jnp.tile(_rope_tables(PAST_LEN + jnp.arange(seq_len, dtype=F32)), (n_seq, 1))
    shared_specs, shared_args = _shared_mixer_operands(w)
    cache_rows = n_seq * SWA_WINDOW
    return pl.pallas_call(
        functools.partial(_mixer_sample_kernel, n_seq=n_seq, seq_len=seq_len),
        grid=(1,),
        in_specs=[pl.BlockSpec(memory_space=pltpu.SMEM), _resident((rows, D_MODEL)),
                  _resident((rows, N_ROPE_TABLES * LANES))]
                 + shared_specs + [_resident((cache_rows, KV_WIDTH)), _resident((cache_rows, KV_WIDTH))],
        out_specs=[_resident((rows, D_MODEL)), _resident((rows, KV_WIDTH)), _resident((rows, KV_WIDTH)),
                   _resident((rows, D_MODEL))],
        out_shape=[jax.ShapeDtypeStruct((rows, D_MODEL), F32),
                   jax.ShapeDtypeStruct((rows, KV_WIDTH), F32),
                   jax.ShapeDtypeStruct((rows, KV_WIDTH), F32),
                   jax.ShapeDtypeStruct((rows, D_MODEL), F32)],
        scratch_shapes=[
            pltpu.VMEM((rows, D_MODEL), BF16),
            pltpu.VMEM((rows, D_MODEL), BF16),
            pltpu.VMEM((rows, D_MODEL), BF16),
            pltpu.VMEM((rows, Q_WIDTH), BF16),
            pltpu.VMEM((rows, KV_WIDTH), BF16),
            pltpu.VMEM((rows, KV_WIDTH), BF16),
            pltpu.VMEM((rows, Q_WIDTH), BF16),
        ],
        compiler_params=pltpu.CompilerParams(
            dimension_semantics=("arbitrary",), vmem_limit_bytes=VMEM_LIMIT_BYTES),
        name="mixer_sample",
    )(w["attn_sinks"], x.reshape(rows, D_MODEL), rope, *shared_args,
      cache_k.reshape(cache_rows, KV_WIDTH), cache_v.reshape(cache_rows, KV_WIDTH))


def _ffn(h, w, tile, name):
    rows = h.shape[0]
    return pl.pallas_call(
        _ffn_kernel,
        grid=(rows // tile,),
        in_specs=[pl.BlockSpec((tile, D_MODEL), lambda i: (i, 0)),
                  _resident((1, D_MODEL)), _resident((1, D_MODEL)),
                  _resident((D_MODEL, D_FF)), _resident((D_FF, D_MODEL))],
        out_specs=pl.BlockSpec((tile, D_MODEL), lambda i: (i, 0)),
        out_shape=jax.ShapeDtypeStruct((rows, D_MODEL), F32),
        scratch_shapes=[pltpu.VMEM((tile, D_MODEL), BF16)],
        compiler_params=pltpu.CompilerParams(
            dimension_semantics=("arbitrary",), vmem_limit_bytes=VMEM_LIMIT_BYTES),
        name=name,
    )(h, w["g_ffn_pre"], w["g_ffn_post"], w["w_ff1"], w["w_ff2"])


def _layer(h_p, h_s, ck, cv, w):
    B, S, _ = h_p.shape
    n_seq, seq_len, _ = h_s.shape
    h_p, k_last, v_last = _mixer_prompt(h_p, w)
    y_p = _ffn(h_p.reshape(B * S, D_MODEL), w, FFN_TILE, "ffn_prompt").reshape(B, S, D_MODEL)
    h_s, k_new, v_new, v_sgu = _mixer_sample(h_s, ck, cv, w)
    y_s = _ffn(h_s, w, n_seq * seq_len, "ffn_sample").reshape(n_seq, seq_len, D_MODEL)
    return (y_p, y_s,
            k_last.reshape(B, SWA_WINDOW, KV_HEADS, HEAD_DIM), v_last.reshape(B, SWA_WINDOW, KV_HEADS, HEAD_DIM),
            k_new.reshape(n_seq, seq_len, KV_HEADS, HEAD_DIM), v_new.reshape(n_seq, seq_len, KV_HEADS, HEAD_DIM),
            v_sgu.reshape(n_seq, seq_len, D_MODEL))


def kernel(x_prompt, x_sample, cache_swa_k, cache_swa_v, w_in, sgu_ln_g, sgu_ln_b, sgu_w, sgu_b, attn_sinks,
           w_branch_a, w_branch_b, w_out, g_mix_pre, g_mix_post, g_ffn_pre, g_ffn_post, w_ff1, w_ff2):
    depth = w_in.shape[0]
    h_p, h_s = x_prompt, x_sample
    per_layer = []
    for l in range(depth):
        row = lambda v: v[l].reshape(1, -1)
        w_in_l = w_in[l].astype(BF16)
        w_in_l = jnp.concatenate([w_in_l[:, :OFF_Q], _q_heads_by_group(w_in_l[:, OFF_Q:OFF_K], 1),
                                  w_in_l[:, OFF_K:]], axis=1)
        w = {
            "w_in": w_in_l, "w_branch_a": w_branch_a[l].astype(BF16),
            "w_branch_b": _q_heads_by_group(w_branch_b[l].astype(BF16), 0), "w_out": w_out[l].astype(BF16),
            "w_ff1": w_ff1[l].astype(BF16), "w_ff2": w_ff2[l].astype(BF16),
            "sgu_ln_g": row(sgu_ln_g), "sgu_ln_b": row(sgu_ln_b), "sgu_w": sgu_w[l],
            "sgu_b_rows": jnp.repeat(sgu_b[l].T, SGU_GROUP_DIM, axis=1),
            "attn_sinks": attn_sinks[l],
            "g_mix_pre": row(g_mix_pre), "g_mix_post": row(g_mix_post),
            "g_ffn_pre": row(g_ffn_pre), "g_ffn_post": row(g_ffn_post),
        }
        h_p, h_s, *states = _layer(h_p, h_s, cache_swa_k[l], cache_swa_v[l], w)
        per_layer.append(states)
    stacked = [jnp.stack([states[i] for states in per_layer]) for i in range(5)]
    return (h_p, h_s, *stacked)
```

```python
import functools

import numpy as np
import jax
import jax.numpy as jnp
from jax import lax
from jax.experimental import pallas as pl
from jax.experimental.pallas import tpu as pltpu

D_MODEL = 1024
CHUNK = 64
SGU_CHUNK = 128
SGU_GROUPS = 8
SGU_GROUP_DIM = D_MODEL // SGU_GROUPS
N_HEADS = 16
KV_HEADS = 4
HEAD_DIM = 64
Q_GROUP = N_HEADS // KV_HEADS
SWA_WINDOW = 128
ROT_DIM = HEAD_DIM // 4
ROPE_THETA = 500000.0
D_FF = 4 * D_MODEL
NORM_EPS = 1e-6
PAST_LEN = 4096
MASK_VALUE = -1e30

Q_WIDTH = N_HEADS * HEAD_DIM
KV_WIDTH = KV_HEADS * HEAD_DIM
OFF_U = 0
OFF_V = OFF_U + D_MODEL
OFF_Q = OFF_V + D_MODEL
OFF_K = OFF_Q + Q_WIDTH
OFF_VA = OFF_K + KV_WIDTH
OFF_GA = OFF_VA + KV_WIDTH
OFF_GB = OFF_GA + D_MODEL
IN_WIDTH = OFF_GB + D_MODEL

LANES = 128
MXU_COLS = 256
VMEM_LIMIT_BYTES = 52 * 1024 * 1024
MIX_TILE = 512
MIX_SUB = MIX_TILE
FFN_TILE = 1024
FFN_SUB = 512
ATT_BLOCK = 2 * CHUNK
ATT_KEYS = ATT_BLOCK + SWA_WINDOW
FF_CHUNK = 2048
SAMPLE_UNROLL = 4

F32 = jnp.float32
BF16 = jnp.bfloat16
SQRT_HALF = np.sqrt(0.5).astype(np.float32)
LOG2_E = np.float32(np.log2(np.e))
SCORE_SCALE = np.float32(HEAD_DIM ** -0.5) * LOG2_E
N_ROPE_TABLES = 4


def _dot(a, b):
    return jnp.dot(a, b, preferred_element_type=F32)


def _dot_nt(a, b):
    return lax.dot_general(a, b, (((1,), (1,)), ((), ())), preferred_element_type=F32)


def _rmsnorm(x, g):
    return x * lax.rsqrt(jnp.mean(x * x, axis=-1, keepdims=True) + NORM_EPS) * g


def _layernorm(x, g, b):
    xc = x - jnp.mean(x, axis=-1, keepdims=True)
    return xc * lax.rsqrt(jnp.mean(xc * xc, axis=-1, keepdims=True) + NORM_EPS) * g + b


def _gelu(x):
    return 0.5 * x * (1.0 + lax.erf(x * SQRT_HALF))


def _rope(x, cos, sin):
    half = ROT_DIM // 2
    dim = lax.broadcasted_iota(jnp.int32, (x.shape[0], LANES), 1) % HEAD_DIM
    second_half = (dim >= half) & (dim < ROT_DIM)
    out = []
    for j in range(x.shape[1] // LANES):
        xj = x[:, j * LANES:(j + 1) * LANES]
        partner = jnp.where(second_half, pltpu.roll(xj, half, axis=1), pltpu.roll(xj, LANES - half, axis=1))
        out.append(xj * cos + partner * sin)
    return jnp.concatenate(out, axis=1)


def _split_rope_tables(rope):
    tables = [rope[:, j * LANES:(j + 1) * LANES] for j in range(N_ROPE_TABLES)]
    return tables[:N_ROPE_TABLES // 2], tables[N_ROPE_TABLES // 2:]


def _head_lane_masks_bf16():
    lane = np.arange(KV_WIDTH)
    masks = np.stack([(lane // HEAD_DIM == h) for h in range(KV_HEADS)]).astype(np.float32)
    return jnp.asarray(np.broadcast_to(masks[:, None, :], (KV_HEADS, ATT_BLOCK, KV_WIDTH)), dtype=BF16)


def _masked_sgu_weights(sguw_ref, size):
    i = lax.broadcasted_iota(jnp.int32, (size, size), 0) // CHUNK
    j = lax.broadcasted_iota(jnp.int32, (size, size), 1) // CHUNK
    keep = i >= j
    return [jnp.where(keep, sguw_ref[g, :size, :size], 0.0).astype(BF16) for g in range(SGU_GROUPS)]


def _stack_heads(q_bf, lane_masks_bf):
    return jnp.concatenate([q_bf * lane_masks_bf[h] for h in range(KV_HEADS)], axis=0)


def _band_mask(s):
    half = ATT_KEYS // 2
    lane = lax.broadcasted_iota(jnp.int32, (CHUNK, half), 1)
    top = jnp.concatenate([s[:CHUNK, :half], jnp.where(lane < CHUNK, s[:CHUNK, half:], MASK_VALUE)], axis=1)
    bot = jnp.concatenate([jnp.where(lane >= CHUNK, s[CHUNK:, :half], MASK_VALUE), s[CHUNK:, half:]], axis=1)
    return jnp.concatenate([top, bot], axis=0)


def _attention_probs(q_stack, k_win, mask, sinks, rows):
    s = _dot_nt(q_stack, k_win)
    es, invs = [], []
    for n, sink in enumerate(sinks):
        sh = mask(s[n * rows:(n + 1) * rows])
        m = jnp.maximum(jnp.max(sh, axis=-1, keepdims=True), sink)
        e = jnp.exp2(sh - m)
        den = jnp.sum(e, axis=-1, keepdims=True) + jnp.exp2(sink - m)
        es.append(e.astype(BF16))
        invs.append(1.0 / den)
    return jnp.concatenate(es, axis=0), invs


def _merge_heads(o, invs, rows):
    lane = lax.broadcasted_iota(jnp.int32, (rows, KV_WIDTH), 1)
    out = o[(KV_HEADS - 1) * rows:] * invs[KV_HEADS - 1]
    for h in range(KV_HEADS - 2, -1, -1):
        out = jnp.where(lane < (h + 1) * HEAD_DIM, o[h * rows:(h + 1) * rows] * invs[h], out)
    return out


def _attention_output(e_stack, invs, v_win, rows):
    return _merge_heads(_dot(e_stack, v_win), invs, rows)


def _merge_and_residual(x, xn_ref, a_ref, b_ref, win_ref, wa_ref, wb_ref, wo_ref, gpost):
    ga = jax.nn.sigmoid(_dot(xn_ref[...], win_ref[:, OFF_GA:OFF_GA + D_MODEL]))
    m = ga * _dot(a_ref[...], wa_ref[...])
    gb = jax.nn.sigmoid(_dot(xn_ref[...], win_ref[:, OFF_GB:OFF_GB + D_MODEL]))
    m = m + gb * _dot(b_ref[...], wb_ref[...])
    out = _dot(m.astype(BF16), wo_ref[...])
    return x + _rmsnorm(out, gpost)


def _mixer_prompt_kernel(sinks_ref, x_ref, rope_ref, gpre_ref, gpost_ref, win_ref,
                         lng_ref, lnb_ref, sguw_ref, sgub_ref, qmask_ref, wa_ref, wb_ref, wo_ref,
                         h_ref, klast_ref, vlast_ref,
                         xn_ref, gv_ref, vn_ref, a_ref, q_ref, kext_ref, vext_ref, b_ref):
    T = x_ref.shape[0]
    t = pl.program_id(1)

    @pl.when(t == 0)
    def _():
        kext_ref[0:SWA_WINDOW] = jnp.zeros((SWA_WINDOW, KV_WIDTH), BF16)
        vext_ref[0:SWA_WINDOW] = jnp.zeros((SWA_WINDOW, KV_WIDTH), BF16)

    @pl.when(t > 0)
    def _():
        kext_ref[0:SWA_WINDOW] = kext_ref[T:T + SWA_WINDOW]
        vext_ref[0:SWA_WINDOW] = vext_ref[T:T + SWA_WINDOW]

    w_sgu = _masked_sgu_weights(sguw_ref, SGU_CHUNK)
    lane_masks_bf = [qmask_ref[h] for h in range(KV_HEADS)]
    q_chunk = lax.broadcasted_iota(jnp.int32, (ATT_BLOCK, ATT_KEYS), 0) // CHUNK
    key_idx = lax.broadcasted_iota(jnp.int32, (ATT_BLOCK, ATT_KEYS), 1)
    key_chunk = key_idx // CHUNK
    band = (key_chunk >= q_chunk) & (key_chunk <= q_chunk + SWA_WINDOW // CHUNK)
    has_prev = jnp.where(t > 0, SWA_WINDOW, 0)
    band_first = band & (key_idx + has_prev >= SWA_WINDOW)
    first_block_mask = lambda s: jnp.where(band_first, s, MASK_VALUE)

    def norm_in(r0):
        rows = slice(r0, r0 + MIX_SUB)
        xn_ref[rows] = _rmsnorm(x_ref[rows], gpre_ref[...]).astype(BF16)

    def gelu_v(r0):
        rows = slice(r0, r0 + MIX_SUB)
        for c in range(D_MODEL // MXU_COLS):
            cols = slice(c * MXU_COLS, (c + 1) * MXU_COLS)
            gv_ref[rows, cols] = _gelu(
                _dot(xn_ref[rows], win_ref[:, OFF_V + c * MXU_COLS:OFF_V + (c + 1) * MXU_COLS]))

    def attention_inputs(r0):
        rows = slice(r0, r0 + MIX_SUB)
        k_tables, q_tables = _split_rope_tables(rope_ref[rows])
        q_ref[rows] = _rope(_dot(xn_ref[rows], win_ref[:, OFF_Q:OFF_Q + Q_WIDTH]), *q_tables).astype(BF16)
        kva = _dot(xn_ref[rows], win_ref[:, OFF_K:OFF_VA + KV_WIDTH])
        k = _rope(kva[:, :KV_WIDTH], *k_tables)
        va = kva[:, KV_WIDTH:]
        if r0 + MIX_SUB == T:
            klast_ref[...] = k[MIX_SUB - SWA_WINDOW:]
            vlast_ref[...] = va[MIX_SUB - SWA_WINDOW:]
        kext_ref[SWA_WINDOW + r0:SWA_WINDOW + r0 + MIX_SUB] = k.astype(BF16)
        vext_ref[SWA_WINDOW + r0:SWA_WINDOW + r0 + MIX_SUB] = va.astype(BF16)

    def norm_v(r0):
        rows = slice(r0, r0 + MIX_SUB)
        vn_ref[rows] = _layernorm(gv_ref[rows], lng_ref[...], lnb_ref[...]).astype(BF16)

    def gating(r0):
        n_chunks = MIX_SUB // SGU_CHUNK
        groups_per_step = MXU_COLS // SGU_GROUP_DIM
        for c in range(D_MODEL // MXU_COLS):
            gu = _gelu(_dot(xn_ref[r0:r0 + MIX_SUB],
                            win_ref[:, OFF_U + c * MXU_COLS:OFF_U + (c + 1) * MXU_COLS]))
            for gg in range(groups_per_step):
                g = c * groups_per_step + gg
                gcols = slice(g * SGU_GROUP_DIM, (g + 1) * SGU_GROUP_DIM)
                rhs = jnp.concatenate(
                    [vn_ref[r0 + n * SGU_CHUNK:r0 + (n + 1) * SGU_CHUNK, gcols] for n in range(n_chunks)], axis=1)
                s = _dot(w_sgu[g], rhs)
                bias = sgub_ref[:, gcols]
                for n in range(n_chunks):
                    sn = s[:, n * SGU_GROUP_DIM:(n + 1) * SGU_GROUP_DIM] + bias
                    gun = gu[n * SGU_CHUNK:(n + 1) * SGU_CHUNK, gg * SGU_GROUP_DIM:(gg + 1) * SGU_GROUP_DIM]
                    a_ref[r0 + n * SGU_CHUNK:r0 + (n + 1) * SGU_CHUNK, gcols] = (gun * sn).astype(BF16)

    def attention(r0):
        blocks = [(i, g) for i in range(r0 // ATT_BLOCK, (r0 + MIX_SUB) // ATT_BLOCK) for g in range(Q_GROUP)]

        def probs(i, g):
            rows = slice(i * ATT_BLOCK, (i + 1) * ATT_BLOCK)
            krows = slice(i * ATT_BLOCK, i * ATT_BLOCK + ATT_KEYS)
            sinks = [sinks_ref[h * Q_GROUP + g] * LOG2_E for h in range(KV_HEADS)]
            q_stack = _stack_heads(q_ref[rows, g * KV_WIDTH:(g + 1) * KV_WIDTH], lane_masks_bf)
            mask = first_block_mask if i == 0 else _band_mask
            return _attention_probs(q_stack, kext_ref[krows, :], mask, sinks, ATT_BLOCK)

        def output(i, g, e_stack, invs):
            rows = slice(i * ATT_BLOCK, (i + 1) * ATT_BLOCK)
            krows = slice(i * ATT_BLOCK, i * ATT_BLOCK + ATT_KEYS)
            o = _attention_output(e_stack, invs, vext_ref[krows, :], ATT_BLOCK)
            b_ref[rows, g * KV_WIDTH:(g + 1) * KV_WIDTH] = o.astype(BF16)

        for block in blocks:
            output(*block, *probs(*block))

    def merge(r0):
        rows = slice(r0, r0 + MIX_SUB)
        ga = jax.nn.sigmoid(_dot(xn_ref[rows], win_ref[:, OFF_GA:OFF_GA + D_MODEL]))
        m = ga * _dot(a_ref[rows], wa_ref[...])
        gb = jax.nn.sigmoid(_dot(xn_ref[rows], win_ref[:, OFF_GB:OFF_GB + D_MODEL]))
        m = m + gb * _dot(b_ref[rows], wb_ref[...])
        out = _dot(m.astype(BF16), wo_ref[...])
        h_ref[rows] = x_ref[rows] + _rmsnorm(out, gpost_ref[...])

    for phase in (norm_in, gelu_v, norm_v, gating, attention_inputs, attention, merge):
        for r0 in range(0, T, MIX_SUB):
            phase(r0)


def _sample_layer_kernel(sinks_ref, x_ref, rope_ref, gpre_ref, gpost_ref, win_ref,
                         lng_ref, lnb_ref, sguw_ref, sgub_ref, qmask_ref, wa_ref, wb_ref, wo_ref,
                         ck_ref, cv_ref, gfpre_ref, gfpost_ref, w1_ref, w2_ref,
                         y_ref, knew_ref, vnew_ref, vsgu_ref,
                         xn_ref, vn_ref, a_ref, q_ref, kn_ref, vnew_bf_ref, b_ref,
                         *, n_seq, seq_len):
    x = x_ref[...]
    xn_ref[...] = _rmsnorm(x, gpre_ref[...]).astype(BF16)

    vn = _layernorm(_gelu(_dot(xn_ref[...], win_ref[:, OFF_V:OFF_V + D_MODEL])), lng_ref[...], lnb_ref[...])
    vsgu_ref[...] = vn
    vn_ref[...] = vn.astype(BF16)
    gu = _gelu(_dot(xn_ref[...], win_ref[:, OFF_U:OFF_U + D_MODEL]))
    w_sgu = _masked_sgu_weights(sguw_ref, seq_len)
    for g in range(SGU_GROUPS):
        gcols = slice(g * SGU_GROUP_DIM, (g + 1) * SGU_GROUP_DIM)
        rhs = jnp.concatenate([vn_ref[n * seq_len:(n + 1) * seq_len, gcols] for n in range(n_seq)], axis=1)
        s = _dot(w_sgu[g], rhs)
        bias = sgub_ref[0:seq_len, gcols]
        for n in range(n_seq):
            rows = slice(n * seq_len, (n + 1) * seq_len)
            sn = s[:, n * SGU_GROUP_DIM:(n + 1) * SGU_GROUP_DIM] + bias
            a_ref[rows, gcols] = (gu[rows, gcols] * sn).astype(BF16)

    k_tables, q_tables = _split_rope_tables(rope_ref[...])
    q_ref[...] = _rope(_dot(xn_ref[...], win_ref[:, OFF_Q:OFF_Q + Q_WIDTH]), *q_tables).astype(BF16)
    k = _rope(_dot(xn_ref[...], win_ref[:, OFF_K:OFF_K + KV_WIDTH]), *k_tables)
    va = _dot(xn_ref[...], win_ref[:, OFF_VA:OFF_VA + KV_WIDTH])
    knew_ref[...] = k
    vnew_ref[...] = va
    kn_ref[...] = k.astype(BF16)
    vnew_bf_ref[...] = va.astype(BF16)

    n_keys = SWA_WINDOW + seq_len
    pad = jnp.zeros((ATT_KEYS - n_keys, KV_WIDTH), BF16)
    lane_masks_bf = [qmask_ref[h, 0:seq_len, :] for h in range(KV_HEADS)]
    real_key = lax.broadcasted_iota(jnp.int32, (seq_len, ATT_KEYS), 1) < n_keys
    mask = lambda s: jnp.where(real_key, s, MASK_VALUE)

    def per_sequence(n, carry):
        new_rows = pl.ds(pl.multiple_of(n * seq_len, seq_len), seq_len)
        old_rows = pl.ds(pl.multiple_of(n * SWA_WINDOW, SWA_WINDOW), SWA_WINDOW)
        k_win = jnp.concatenate([ck_ref[old_rows, :].astype(BF16), kn_ref[new_rows, :], pad], axis=0)
        v_win = jnp.concatenate([cv_ref[old_rows, :].astype(BF16), vnew_bf_ref[new_rows, :], pad], axis=0)
        q_stack = jnp.concatenate(
            [_stack_heads(q_ref[new_rows, g * KV_WIDTH:(g + 1) * KV_WIDTH], lane_masks_bf) for g in range(Q_GROUP)],
            axis=0)
        sinks = [sinks_ref[h * Q_GROUP + g] * LOG2_E for g in range(Q_GROUP) for h in range(KV_HEADS)]
        e_stack, invs = _attention_probs(q_stack, k_win, mask, sinks, seq_len)
        o = _dot(e_stack, v_win)
        group_rows = KV_HEADS * seq_len
        for g in range(Q_GROUP):
            b_ref[new_rows, g * KV_WIDTH:(g + 1) * KV_WIDTH] = _merge_heads(
                o[g * group_rows:(g + 1) * group_rows], invs[g * KV_HEADS:(g + 1) * KV_HEADS], seq_len).astype(BF16)
        return carry

    lax.fori_loop(0, n_seq, per_sequence, 0, unroll=SAMPLE_UNROLL)

    h = _merge_and_residual(x, xn_ref, a_ref, b_ref, win_ref, wa_ref, wb_ref, wo_ref, gpost_ref[...])
    y_ref[...] = _ffn_rows(h, gfpre_ref[...], gfpost_ref[...], w1_ref, w2_ref)


def _ffn_rows(h, gpre, gpost, w1_ref, w2_ref):
    hn = _rmsnorm(h, gpre).astype(BF16)
    z = None
    for j in range(D_FF // FF_CHUNK):
        f = _dot(hn, w1_ref[:, j * FF_CHUNK:(j + 1) * FF_CHUNK])
        f = jnp.square(jnp.maximum(f, 0.0)).astype(BF16)
        zj = _dot(f, w2_ref[j * FF_CHUNK:(j + 1) * FF_CHUNK, :])
        z = zj if z is None else z + zj
    return h + _rmsnorm(z, gpost)


def _ffn_kernel(h_ref, gpre_ref, gpost_ref, w1_ref, w2_ref, y_ref):
    for r0 in range(0, h_ref.shape[0], FFN_SUB):
        rows = slice(r0, r0 + FFN_SUB)
        y_ref[rows] = _ffn_rows(h_ref[rows], gpre_ref[...], gpost_ref[...], w1_ref, w2_ref)


def _resident(shape):
    return pl.BlockSpec(shape, lambda *_: (0,) * len(shape), pipeline_mode=pl.Buffered(1))


def _rope_tables(pos):
    half = ROT_DIM // 2
    inv = ROPE_THETA ** (-np.arange(half, dtype=np.float64) * 2.0 / ROT_DIM)
    ang = np.asarray(pos, np.float64)[:, None] * inv[None, :]
    cos, sin = np.cos(ang), np.sin(ang)
    n = ang.shape[0]
    cos_t = np.concatenate([cos, cos, np.ones((n, HEAD_DIM - ROT_DIM))], axis=1)
    sin_t = np.concatenate([-sin, sin, np.zeros((n, HEAD_DIM - ROT_DIM))], axis=1)
    tile = lambda t: np.tile(t, (1, LANES // HEAD_DIM))
    k_tables = [tile(cos_t), tile(sin_t)]
    return np.concatenate(k_tables + [t * np.float64(SCORE_SCALE) for t in k_tables], axis=1).astype(np.float32)


def _q_heads_by_group(w, axis):
    shape = w.shape
    split = shape[:axis] + (KV_HEADS, Q_GROUP, HEAD_DIM) + shape[axis + 1:]
    return jnp.swapaxes(w.reshape(split), axis, axis + 1).reshape(shape)


def _shared_mixer_operands(w):
    specs = [
        _resident((1, D_MODEL)), _resident((1, D_MODEL)), _resident((D_MODEL, IN_WIDTH)),
        _resident((1, D_MODEL)), _resident((1, D_MODEL)),
        _resident((SGU_GROUPS, SGU_CHUNK, SGU_CHUNK)), _resident((SGU_CHUNK, D_MODEL)),
        _resident((KV_HEADS, ATT_BLOCK, KV_WIDTH)),
        _resident((D_MODEL, D_MODEL)), _resident((Q_WIDTH, D_MODEL)), _resident((D_MODEL, D_MODEL)),
    ]
    args = [w["g_mix_pre"], w["g_mix_post"], w["w_in"], w["sgu_ln_g"], w["sgu_ln_b"], w["sgu_w"],
            w["sgu_b_rows"], _head_lane_masks_bf16(), w["w_branch_a"], w["w_branch_b"], w["w_out"]]
    return specs, args


def _mixer_prompt(x, w):
    B, S, _ = x.shape
    T = MIX_TILE
    rope = jnp.asarray(_rope_tables(np.arange(S)))
    table_spec = pl.BlockSpec((T, N_ROPE_TABLES * LANES), lambda b, t: (t, 0))
    shared_specs, shared_args = _shared_mixer_operands(w)
    last_spec = pl.BlockSpec((None, SWA_WINDOW, KV_WIDTH), lambda b, t: (b, 0, 0))
    return pl.pallas_call(
        _mixer_prompt_kernel,
        grid=(B, S // T),
        in_specs=[pl.BlockSpec(memory_space=pltpu.SMEM),
                  pl.BlockSpec((None, T, D_MODEL), lambda b, t: (b, t, 0)),
                  table_spec] + shared_specs,
        out_specs=[pl.BlockSpec((None, T, D_MODEL), lambda b, t: (b, t, 0)), last_spec, last_spec],
        out_shape=[jax.ShapeDtypeStruct((B, S, D_MODEL), F32),
                   jax.ShapeDtypeStruct((B, SWA_WINDOW, KV_WIDTH), F32),
                   jax.ShapeDtypeStruct((B, SWA_WINDOW, KV_WIDTH), F32)],
        scratch_shapes=[
            pltpu.VMEM((T, D_MODEL), BF16),
            pltpu.VMEM((T, D_MODEL), F32),
            pltpu.VMEM((T, D_MODEL), BF16),
            pltpu.VMEM((T, D_MODEL), BF16),
            pltpu.VMEM((T, Q_WIDTH), BF16),
            pltpu.VMEM((T + SWA_WINDOW, KV_WIDTH), BF16),
            pltpu.VMEM((T + SWA_WINDOW, KV_WIDTH), BF16),
            pltpu.VMEM((T, Q_WIDTH), BF16),
        ],
        compiler_params=pltpu.CompilerParams(
            dimension_semantics=("arbitrary", "arbitrary"), vmem_limit_bytes=VMEM_LIMIT_BYTES),
        name="mixer_prompt",
    )(w["attn_sinks"], x, rope, *shared_args)


def _sample_layer(x, cache_k, cache_v, w):
    n_seq, seq_len, _ = x.shape
    rows = n_seq * seq_len
    rope = jnp.asarray(np.tile(_rope_tables(PAST_LEN + np.arange(seq_len)), (n_seq, 1)))
    shared_specs, shared_args = _shared_mixer_operands(w)
    cache_rows = n_seq * SWA_WINDOW
    return pl.pallas_call(
        functools.partial(_sample_layer_kernel, n_seq=n_seq, seq_len=seq_len),
        grid=(1,),
        in_specs=[pl.BlockSpec(memory_space=pltpu.SMEM), _resident((rows, D_MODEL)),
                  _resident((rows, N_ROPE_TABLES * LANES))]
                 + shared_specs + [_resident((cache_rows, KV_WIDTH)), _resident((cache_rows, KV_WIDTH))]
                 + _ffn_weight_specs(),
        out_specs=[_resident((rows, D_MODEL)), _resident((rows, KV_WIDTH)), _resident((rows, KV_WIDTH)),
                   _resident((rows, D_MODEL))],
        out_shape=[jax.ShapeDtypeStruct((rows, D_MODEL), F32),
                   jax.ShapeDtypeStruct((rows, KV_WIDTH), F32),
                   jax.ShapeDtypeStruct((rows, KV_WIDTH), F32),
                   jax.ShapeDtypeStruct((rows, D_MODEL), F32)],
        scratch_shapes=[
            pltpu.VMEM((rows, D_MODEL), BF16),
            pltpu.VMEM((rows, D_MODEL), BF16),
            pltpu.VMEM((rows, D_MODEL), BF16),
            pltpu.VMEM((rows, Q_WIDTH), BF16),
            pltpu.VMEM((rows, KV_WIDTH), BF16),
            pltpu.VMEM((rows, KV_WIDTH), BF16),
            pltpu.VMEM((rows, Q_WIDTH), BF16),
        ],
        compiler_params=pltpu.CompilerParams(
            dimension_semantics=("arbitrary",), vmem_limit_bytes=VMEM_LIMIT_BYTES),
        name="sample_layer",
    )(w["attn_sinks"], x.reshape(rows, D_MODEL), rope, *shared_args,
      cache_k.reshape(cache_rows, KV_WIDTH), cache_v.reshape(cache_rows, KV_WIDTH), *_ffn_weight_args(w))


def _ffn_weight_specs():
    return [_resident((1, D_MODEL)), _resident((1, D_MODEL)),
            _resident((D_MODEL, D_FF)), _resident((D_FF, D_MODEL))]


def _ffn_weight_args(w):
    return [w["g_ffn_pre"], w["g_ffn_post"], w["w_ff1"], w["w_ff2"]]


def _ffn_prompt(h, w):
    rows = h.shape[0]
    return pl.pallas_call(
        _ffn_kernel,
        grid=(rows // FFN_TILE,),
        in_specs=[pl.BlockSpec((FFN_TILE, D_MODEL), lambda i: (i, 0))] + _ffn_weight_specs(),
        out_specs=pl.BlockSpec((FFN_TILE, D_MODEL), lambda i: (i, 0)),
        out_shape=jax.ShapeDtypeStruct((rows, D_MODEL), F32),
        compiler_params=pltpu.CompilerParams(
            dimension_semantics=("arbitrary",), vmem_limit_bytes=VMEM_LIMIT_BYTES),
        name="ffn_prompt",
    )(h, *_ffn_weight_args(w))


def _layer(h_p, h_s, ck, cv, w):
    B, S, _ = h_p.shape
    n_seq, seq_len, _ = h_s.shape
    h_p, k_last, v_last = _mixer_prompt(h_p, w)
    y_p = _ffn_prompt(h_p.reshape(B * S, D_MODEL), w).reshape(B, S, D_MODEL)
    y_s, k_new, v_new, v_sgu = _sample_layer(h_s, ck, cv, w)
    y_s = y_s.reshape(n_seq, seq_len, D_MODEL)
    return (y_p, y_s,
            k_last.reshape(B, SWA_WINDOW, KV_HEADS, HEAD_DIM), v_last.reshape(B, SWA_WINDOW, KV_HEADS, HEAD_DIM),
            k_new.reshape(n_seq, seq_len, KV_HEADS, HEAD_DIM), v_new.reshape(n_seq, seq_len, KV_HEADS, HEAD_DIM),
            v_sgu.reshape(n_seq, seq_len, D_MODEL))


def kernel(x_prompt, x_sample, cache_swa_k, cache_swa_v, w_in, sgu_ln_g, sgu_ln_b, sgu_w, sgu_b, attn_sinks,
           w_branch_a, w_branch_b, w_out, g_mix_pre, g_mix_post, g_ffn_pre, g_ffn_post, w_ff1, w_ff2):
    depth = w_in.shape[0]
    h_p, h_s = x_prompt, x_sample
    per_layer = []
    for l in range(depth):
        row = lambda v: v[l].reshape(1, -1)
        w_in_l = jnp.concatenate([w_in[l][:, :OFF_Q], _q_heads_by_group(w_in[l][:, OFF_Q:OFF_K], 1),
                                  w_in[l][:, OFF_K:]], axis=1).astype(BF16)
        w = {
            "w_in": w_in_l, "w_branch_a": w_branch_a[l].astype(BF16),
            "w_branch_b": _q_heads_by_group(w_branch_b[l], 0).astype(BF16), "w_out": w_out[l].astype(BF16),
            "w_ff1": w_ff1[l].astype(BF16), "w_ff2": w_ff2[l].astype(BF16),
            "sgu_ln_g": row(sgu_ln_g), "sgu_ln_b": row(sgu_ln_b), "sgu_w": sgu_w[l],
            "sgu_b_rows": jnp.repeat(sgu_b[l].T, SGU_GROUP_DIM, axis=1),
            "attn_sinks": attn_sinks[l],
            "g_mix_pre": row(g_mix_pre), "g_mix_post": row(g_mix_post),
            "g_ffn_pre": row(g_ffn_pre), "g_ffn_post": row(g_ffn_post),
        }
        h_p, h_s, *states = _layer(h_p, h_s, cache_swa_k[l], cache_swa_v[l], w)
        per_layer.append(states)
    stacked = [jnp.stack([states[i] for states in per_layer]) for i in range(5)]
    return (h_p, h_s, *stacked)
```

```python
import functools

import numpy as np
import jax
import jax.numpy as jnp
from jax import lax
from jax.experimental import pallas as pl
from jax.experimental.pallas import tpu as pltpu

D_MODEL = 1024
CHUNK = 64
SGU_CHUNK = 128
SGU_GROUPS = 8
SGU_GROUP_DIM = D_MODEL // SGU_GROUPS
N_HEADS = 16
KV_HEADS = 4
HEAD_DIM = 64
Q_GROUP = N_HEADS // KV_HEADS
SWA_WINDOW = 128
ROT_DIM = HEAD_DIM // 4
ROPE_THETA = 500000.0
D_FF = 4 * D_MODEL
NORM_EPS = 1e-6
PAST_LEN = 4096
MASK_VALUE = -1e30

Q_WIDTH = N_HEADS * HEAD_DIM
KV_WIDTH = KV_HEADS * HEAD_DIM
OFF_U = 0
OFF_V = OFF_U + D_MODEL
OFF_Q = OFF_V + D_MODEL
OFF_K = OFF_Q + Q_WIDTH
OFF_VA = OFF_K + KV_WIDTH
OFF_GA = OFF_VA + KV_WIDTH
OFF_GB = OFF_GA + D_MODEL
IN_WIDTH = OFF_GB + D_MODEL

LANES = 128
MXU_COLS = 256
VMEM_LIMIT_BYTES = 52 * 1024 * 1024
MIX_TILE = 512
MIX_SUB = MIX_TILE
FFN_TILE = 1024
FFN_SUB = 512
ATT_BLOCK = 2 * CHUNK
ATT_KEYS = ATT_BLOCK + SWA_WINDOW
FF_CHUNK = 2048
SAMPLE_UNROLL = 4

F32 = jnp.float32
BF16 = jnp.bfloat16
SQRT_HALF = np.sqrt(0.5).astype(np.float32)
LOG2_E = np.float32(np.log2(np.e))
SCORE_SCALE = np.float32(HEAD_DIM ** -0.5) * LOG2_E
N_ROPE_TABLES = 4


def _dot(a, b):
    return jnp.dot(a, b, preferred_element_type=F32)


def _dot_nt(a, b):
    return lax.dot_general(a, b, (((1,), (1,)), ((), ())), preferred_element_type=F32)


def _rmsnorm(x, g):
    return x * lax.rsqrt(jnp.mean(x * x, axis=-1, keepdims=True) + NORM_EPS) * g


def _layernorm(x, g, b):
    xc = x - jnp.mean(x, axis=-1, keepdims=True)
    return xc * lax.rsqrt(jnp.mean(xc * xc, axis=-1, keepdims=True) + NORM_EPS) * g + b


def _gelu(x):
    return 0.5 * x * (1.0 + lax.erf(x * SQRT_HALF))


def _rope(x, cos, sin):
    half = ROT_DIM // 2
    dim = lax.broadcasted_iota(jnp.int32, (x.shape[0], LANES), 1) % HEAD_DIM
    second_half = (dim >= half) & (dim < ROT_DIM)
    out = []
    for j in range(x.shape[1] // LANES):
        xj = x[:, j * LANES:(j + 1) * LANES]
        partner = jnp.where(second_half, pltpu.roll(xj, half, axis=1), pltpu.roll(xj, LANES - half, axis=1))
        out.append(xj * cos + partner * sin)
    return jnp.concatenate(out, axis=1)


def _split_rope_tables(rope):
    tables = [rope[:, j * LANES:(j + 1) * LANES] for j in range(N_ROPE_TABLES)]
    return tables[:N_ROPE_TABLES // 2], tables[N_ROPE_TABLES // 2:]


def _head_lane_masks_bf16():
    lane = np.arange(KV_WIDTH)
    masks = np.stack([(lane // HEAD_DIM == h) for h in range(KV_HEADS)]).astype(np.float32)
    return jnp.asarray(np.broadcast_to(masks[:, None, :], (KV_HEADS, ATT_BLOCK, KV_WIDTH)), dtype=BF16)


def _masked_sgu_weights(sguw_ref, size):
    i = lax.broadcasted_iota(jnp.int32, (size, size), 0) // CHUNK
    j = lax.broadcasted_iota(jnp.int32, (size, size), 1) // CHUNK
    keep = i >= j
    return [jnp.where(keep, sguw_ref[g, :size, :size], 0.0).astype(BF16) for g in range(SGU_GROUPS)]


def _stack_heads(q_bf, lane_masks_bf):
    return jnp.concatenate([q_bf * lane_masks_bf[h] for h in range(KV_HEADS)], axis=0)


def _band_mask(s):
    half = ATT_KEYS // 2
    lane = lax.broadcasted_iota(jnp.int32, (CHUNK, half), 1)
    top = jnp.concatenate([s[:CHUNK, :half], jnp.where(lane < CHUNK, s[:CHUNK, half:], MASK_VALUE)], axis=1)
    bot = jnp.concatenate([jnp.where(lane >= CHUNK, s[CHUNK:, :half], MASK_VALUE), s[CHUNK:, half:]], axis=1)
    return jnp.concatenate([top, bot], axis=0)


def _attention_probs(q_stack, k_win, mask, sinks, rows):
    s = _dot_nt(q_stack, k_win)
    es, invs = [], []
    for n, sink in enumerate(sinks):
        sh = mask(s[n * rows:(n + 1) * rows])
        m = jnp.maximum(jnp.max(sh, axis=-1, keepdims=True), sink)
        e = jnp.exp2(sh - m)
        den = jnp.sum(e, axis=-1, keepdims=True) + jnp.exp2(sink - m)
        es.append(e.astype(BF16))
        invs.append(1.0 / den)
    return jnp.concatenate(es, axis=0), invs


def _merge_heads(o, invs, rows):
    lane = lax.broadcasted_iota(jnp.int32, (rows, KV_WIDTH), 1)
    out = o[(KV_HEADS - 1) * rows:] * invs[KV_HEADS - 1]
    for h in range(KV_HEADS - 2, -1, -1):
        out = jnp.where(lane < (h + 1) * HEAD_DIM, o[h * rows:(h + 1) * rows] * invs[h], out)
    return out


def _attention_output(e_stack, invs, v_win, rows):
    return _merge_heads(_dot(e_stack, v_win), invs, rows)


def _merge_and_residual(x, xn_ref, a_ref, b_ref, win_ref, wa_ref, wb_ref, wo_ref, gpost):
    ga = jax.nn.sigmoid(_dot(xn_ref[...], win_ref[:, OFF_GA:OFF_GA + D_MODEL]))
    m = ga * _dot(a_ref[...], wa_ref[...])
    gb = jax.nn.sigmoid(_dot(xn_ref[...], win_ref[:, OFF_GB:OFF_GB + D_MODEL]))
    m = m + gb * _dot(b_ref[...], wb_ref[...])
    out = _dot(m.astype(BF16), wo_ref[...])
    return x + _rmsnorm(out, gpost)


def _mixer_prompt_kernel(sinks_ref, x_ref, rope_ref, gpre_ref, gpost_ref, win_ref,
                         lng_ref, lnb_ref, sguw_ref, sgub_ref, qmask_ref, wa_ref, wb_ref, wo_ref,
                         h_ref, klast_ref, vlast_ref,
                         xn_ref, gv_ref, vn_ref, a_ref, q_ref, kext_ref, vext_ref, b_ref):
    T = x_ref.shape[0]
    t = pl.program_id(1)

    @pl.when(t == 0)
    def _():
        kext_ref[0:SWA_WINDOW] = jnp.zeros((SWA_WINDOW, KV_WIDTH), BF16)
        vext_ref[0:SWA_WINDOW] = jnp.zeros((SWA_WINDOW, KV_WIDTH), BF16)

    @pl.when(t > 0)
    def _():
        kext_ref[0:SWA_WINDOW] = kext_ref[T:T + SWA_WINDOW]
        vext_ref[0:SWA_WINDOW] = vext_ref[T:T + SWA_WINDOW]

    w_sgu = _masked_sgu_weights(sguw_ref, SGU_CHUNK)
    lane_masks_bf = [qmask_ref[h] for h in range(KV_HEADS)]
    q_chunk = lax.broadcasted_iota(jnp.int32, (ATT_BLOCK, ATT_KEYS), 0) // CHUNK
    key_idx = lax.broadcasted_iota(jnp.int32, (ATT_BLOCK, ATT_KEYS), 1)
    key_chunk = key_idx // CHUNK
    band = (key_chunk >= q_chunk) & (key_chunk <= q_chunk + SWA_WINDOW // CHUNK)
    has_prev = jnp.where(t > 0, SWA_WINDOW, 0)
    band_first = band & (key_idx + has_prev >= SWA_WINDOW)
    first_block_mask = lambda s: jnp.where(band_first, s, MASK_VALUE)

    def norm_in(r0):
        rows = slice(r0, r0 + MIX_SUB)
        xn_ref[rows] = _rmsnorm(x_ref[rows], gpre_ref[...]).astype(BF16)

    def gelu_v(r0):
        rows = slice(r0, r0 + MIX_SUB)
        for c in range(D_MODEL // MXU_COLS):
            cols = slice(c * MXU_COLS, (c + 1) * MXU_COLS)
            gv_ref[rows, cols] = _gelu(
                _dot(xn_ref[rows], win_ref[:, OFF_V + c * MXU_COLS:OFF_V + (c + 1) * MXU_COLS]))

    def attention_inputs(r0):
        rows = slice(r0, r0 + MIX_SUB)
        k_tables, q_tables = _split_rope_tables(rope_ref[rows])
        q_ref[rows] = _rope(_dot(xn_ref[rows], win_ref[:, OFF_Q:OFF_Q + Q_WIDTH]), *q_tables).astype(BF16)
        kva = _dot(xn_ref[rows], win_ref[:, OFF_K:OFF_VA + KV_WIDTH])
        k = _rope(kva[:, :KV_WIDTH], *k_tables)
        va = kva[:, KV_WIDTH:]
        if r0 + MIX_SUB == T:
            klast_ref[...] = k[MIX_SUB - SWA_WINDOW:]
            vlast_ref[...] = va[MIX_SUB - SWA_WINDOW:]
        kext_ref[SWA_WINDOW + r0:SWA_WINDOW + r0 + MIX_SUB] = k.astype(BF16)
        vext_ref[SWA_WINDOW + r0:SWA_WINDOW + r0 + MIX_SUB] = va.astype(BF16)

    def norm_v(r0):
        rows = slice(r0, r0 + MIX_SUB)
        vn_ref[rows] = _layernorm(gv_ref[rows], lng_ref[...], lnb_ref[...]).astype(BF16)

    def gating(r0):
        n_chunks = MIX_SUB // SGU_CHUNK
        groups_per_step = MXU_COLS // SGU_GROUP_DIM
        for c in range(D_MODEL // MXU_COLS):
            gu = _gelu(_dot(xn_ref[r0:r0 + MIX_SUB],
                            win_ref[:, OFF_U + c * MXU_COLS:OFF_U + (c + 1) * MXU_COLS]))
            for gg in range(groups_per_step):
                g = c * groups_per_step + gg
                gcols = slice(g * SGU_GROUP_DIM, (g + 1) * SGU_GROUP_DIM)
                rhs = jnp.concatenate(
                    [vn_ref[r0 + n * SGU_CHUNK:r0 + (n + 1) * SGU_CHUNK, gcols] for n in range(n_chunks)], axis=1)
                s = _dot(w_sgu[g], rhs)
                bias = sgub_ref[:, gcols]
                for n in range(n_chunks):
                    sn = s[:, n * SGU_GROUP_DIM:(n + 1) * SGU_GROUP_DIM] + bias
                    gun = gu[n * SGU_CHUNK:(n + 1) * SGU_CHUNK, gg * SGU_GROUP_DIM:(gg + 1) * SGU_GROUP_DIM]
                    a_ref[r0 + n * SGU_CHUNK:r0 + (n + 1) * SGU_CHUNK, gcols] = (gun * sn).astype(BF16)

    def attention(r0):
        blocks = [(i, g) for i in range(r0 // ATT_BLOCK, (r0 + MIX_SUB) // ATT_BLOCK) for g in range(Q_GROUP)]

        def probs(i, g):
            rows = slice(i * ATT_BLOCK, (i + 1) * ATT_BLOCK)
            krows = slice(i * ATT_BLOCK, i * ATT_BLOCK + ATT_KEYS)
            sinks = [sinks_ref[h * Q_GROUP + g] * LOG2_E for h in range(KV_HEADS)]
            q_stack = _stack_heads(q_ref[rows, g * KV_WIDTH:(g + 1) * KV_WIDTH], lane_masks_bf)
            mask = first_block_mask if i == 0 else _band_mask
            return _attention_probs(q_stack, kext_ref[krows, :], mask, sinks, ATT_BLOCK)

        def output(i, g, e_stack, invs):
            rows = slice(i * ATT_BLOCK, (i + 1) * ATT_BLOCK)
            krows = slice(i * ATT_BLOCK, i * ATT_BLOCK + ATT_KEYS)
            o = _attention_output(e_stack, invs, vext_ref[krows, :], ATT_BLOCK)
            b_ref[rows, g * KV_WIDTH:(g + 1) * KV_WIDTH] = o.astype(BF16)

        for block in blocks:
            output(*block, *probs(*block))

    def merge(r0):
        rows = slice(r0, r0 + MIX_SUB)
        ga = jax.nn.sigmoid(_dot(xn_ref[rows], win_ref[:, OFF_GA:OFF_GA + D_MODEL]))
        m = ga * _dot(a_ref[rows], wa_ref[...])
        gb = jax.nn.sigmoid(_dot(xn_ref[rows], win_ref[:, OFF_GB:OFF_GB + D_MODEL]))
        m = m + gb * _dot(b_ref[rows], wb_ref[...])
        out = _dot(m.astype(BF16), wo_ref[...])
        h_ref[rows] = x_ref[rows] + _rmsnorm(out, gpost_ref[...])

    for phase in (norm_in, gelu_v, norm_v, gating, attention_inputs, attention, merge):
        for r0 in range(0, T, MIX_SUB):
            phase(r0)


def _mixer_sample_kernel(sinks_ref, x_ref, rope_ref, gpre_ref, gpost_ref, win_ref,
                         lng_ref, lnb_ref, sguw_ref, sgub_ref, qmask_ref, wa_ref, wb_ref, wo_ref,
                         ck_ref, cv_ref,
                         h_ref, knew_ref, vnew_ref, vsgu_ref,
                         xn_ref, vn_ref, a_ref, q_ref, kn_ref, vnew_bf_ref, b_ref,
                         *, n_seq, seq_len):
    x = x_ref[...]
    xn_ref[...] = _rmsnorm(x, gpre_ref[...]).astype(BF16)

    vn = _layernorm(_gelu(_dot(xn_ref[...], win_ref[:, OFF_V:OFF_V + D_MODEL])), lng_ref[...], lnb_ref[...])
    vsgu_ref[...] = vn
    vn_ref[...] = vn.astype(BF16)
    gu = _gelu(_dot(xn_ref[...], win_ref[:, OFF_U:OFF_U + D_MODEL]))
    w_sgu = _masked_sgu_weights(sguw_ref, seq_len)
    for g in range(SGU_GROUPS):
        gcols = slice(g * SGU_GROUP_DIM, (g + 1) * SGU_GROUP_DIM)
        rhs = jnp.concatenate([vn_ref[n * seq_len:(n + 1) * seq_len, gcols] for n in range(n_seq)], axis=1)
        s = _dot(w_sgu[g], rhs)
        bias = sgub_ref[0:seq_len, gcols]
        for n in range(n_seq):
            rows = slice(n * seq_len, (n + 1) * seq_len)
            sn = s[:, n * SGU_GROUP_DIM:(n + 1) * SGU_GROUP_DIM] + bias
            a_ref[rows, gcols] = (gu[rows, gcols] * sn).astype(BF16)

    k_tables, q_tables = _split_rope_tables(rope_ref[...])
    q_ref[...] = _rope(_dot(xn_ref[...], win_ref[:, OFF_Q:OFF_Q + Q_WIDTH]), *q_tables).astype(BF16)
    k = _rope(_dot(xn_ref[...], win_ref[:, OFF_K:OFF_K + KV_WIDTH]), *k_tables)
    va = _dot(xn_ref[...], win_ref[:, OFF_VA:OFF_VA + KV_WIDTH])
    knew_ref[...] = k
    vnew_ref[...] = va
    kn_ref[...] = k.astype(BF16)
    vnew_bf_ref[...] = va.astype(BF16)

    n_keys = SWA_WINDOW + seq_len
    pad = jnp.zeros((ATT_KEYS - n_keys, KV_WIDTH), BF16)
    lane_masks_bf = [qmask_ref[h, 0:seq_len, :] for h in range(KV_HEADS)]
    real_key = lax.broadcasted_iota(jnp.int32, (seq_len, ATT_KEYS), 1) < n_keys
    mask = lambda s: jnp.where(real_key, s, MASK_VALUE)

    def per_sequence(n, carry):
        new_rows = pl.ds(pl.multiple_of(n * seq_len, seq_len), seq_len)
        old_rows = pl.ds(pl.multiple_of(n * SWA_WINDOW, SWA_WINDOW), SWA_WINDOW)
        k_win = jnp.concatenate([ck_ref[old_rows, :].astype(BF16), kn_ref[new_rows, :], pad], axis=0)
        v_win = jnp.concatenate([cv_ref[old_rows, :].astype(BF16), vnew_bf_ref[new_rows, :], pad], axis=0)
        q_stack = jnp.concatenate(
            [_stack_heads(q_ref[new_rows, g * KV_WIDTH:(g + 1) * KV_WIDTH], lane_masks_bf) for g in range(Q_GROUP)],
            axis=0)
        sinks = [sinks_ref[h * Q_GROUP + g] * LOG2_E for g in range(Q_GROUP) for h in range(KV_HEADS)]
        e_stack, invs = _attention_probs(q_stack, k_win, mask, sinks, seq_len)
        o = _dot(e_stack, v_win)
        group_rows = KV_HEADS * seq_len
        for g in range(Q_GROUP):
            b_ref[new_rows, g * KV_WIDTH:(g + 1) * KV_WIDTH] = _merge_heads(
                o[g * group_rows:(g + 1) * group_rows], invs[g * KV_HEADS:(g + 1) * KV_HEADS], seq_len).astype(BF16)
        return carry

    lax.fori_loop(0, n_seq, per_sequence, 0, unroll=SAMPLE_UNROLL)

    h_ref[...] = _merge_and_residual(x, xn_ref, a_ref, b_ref, win_ref, wa_ref, wb_ref, wo_ref, gpost_ref[...])


def _ffn_kernel(h_ref, gpre_ref, gpost_ref, w1_ref, w2_ref, y_ref, hn_ref):
    n_sub = max(1, h_ref.shape[0] // FFN_SUB)
    sub = h_ref.shape[0] // n_sub
    for r in range(n_sub):
        rows = slice(r * sub, (r + 1) * sub)
        h = h_ref[rows]
        hn_ref[rows] = _rmsnorm(h, gpre_ref[...]).astype(BF16)
        z = None
        for j in range(D_FF // FF_CHUNK):
            f = _dot(hn_ref[rows], w1_ref[:, j * FF_CHUNK:(j + 1) * FF_CHUNK])
            f = jnp.square(jnp.maximum(f, 0.0)).astype(BF16)
            zj = _dot(f, w2_ref[j * FF_CHUNK:(j + 1) * FF_CHUNK, :])
            z = zj if z is None else z + zj
        y_ref[rows] = h + _rmsnorm(z, gpost_ref[...])


def _resident(shape):
    return pl.BlockSpec(shape, lambda *_: (0,) * len(shape), pipeline_mode=pl.Buffered(1))


def _rope_tables(pos):
    half = ROT_DIM // 2
    inv = ROPE_THETA ** (-jnp.arange(half, dtype=F32) * 2.0 / ROT_DIM)
    ang = pos[:, None] * inv[None, :]
    cos, sin = jnp.cos(ang), jnp.sin(ang)
    n = pos.shape[0]
    cos_t = jnp.concatenate([cos, cos, jnp.ones((n, HEAD_DIM - ROT_DIM), F32)], axis=1)
    sin_t = jnp.concatenate([-sin, sin, jnp.zeros((n, HEAD_DIM - ROT_DIM), F32)], axis=1)
    tile = lambda t: jnp.tile(t, (1, LANES // HEAD_DIM))
    k_tables = [tile(cos_t), tile(sin_t)]
    return jnp.concatenate(k_tables + [t * SCORE_SCALE for t in k_tables], axis=1)


def _q_heads_by_group(w, axis):
    shape = w.shape
    split = shape[:axis] + (KV_HEADS, Q_GROUP, HEAD_DIM) + shape[axis + 1:]
    return jnp.swapaxes(w.reshape(split), axis, axis + 1).reshape(shape)


def _shared_mixer_operands(w):
    specs = [
        _resident((1, D_MODEL)), _resident((1, D_MODEL)), _resident((D_MODEL, IN_WIDTH)),
        _resident((1, D_MODEL)), _resident((1, D_MODEL)),
        _resident((SGU_GROUPS, SGU_CHUNK, SGU_CHUNK)), _resident((SGU_CHUNK, D_MODEL)),
        _resident((KV_HEADS, ATT_BLOCK, KV_WIDTH)),
        _resident((D_MODEL, D_MODEL)), _resident((Q_WIDTH, D_MODEL)), _resident((D_MODEL, D_MODEL)),
    ]
    args = [w["g_mix_pre"], w["g_mix_post"], w["w_in"], w["sgu_ln_g"], w["sgu_ln_b"], w["sgu_w"],
            w["sgu_b_rows"], _head_lane_masks_bf16(), w["w_branch_a"], w["w_branch_b"], w["w_out"]]
    return specs, args


def _mixer_prompt(x, w):
    B, S, _ = x.shape
    T = MIX_TILE
    rope = _rope_tables(jnp.arange(S, dtype=F32))
    table_spec = pl.BlockSpec((T, N_ROPE_TABLES * LANES), lambda b, t: (t, 0))
    shared_specs, shared_args = _shared_mixer_operands(w)
    last_spec = pl.BlockSpec((None, SWA_WINDOW, KV_WIDTH), lambda b, t: (b, 0, 0))
    return pl.pallas_call(
        _mixer_prompt_kernel,
        grid=(B, S // T),
        in_specs=[pl.BlockSpec(memory_space=pltpu.SMEM),
                  pl.BlockSpec((None, T, D_MODEL), lambda b, t: (b, t, 0)),
                  table_spec] + shared_specs,
        out_specs=[pl.BlockSpec((None, T, D_MODEL), lambda b, t: (b, t, 0)), last_spec, last_spec],
        out_shape=[jax.ShapeDtypeStruct((B, S, D_MODEL), F32),
                   jax.ShapeDtypeStruct((B, SWA_WINDOW, KV_WIDTH), F32),
                   jax.ShapeDtypeStruct((B, SWA_WINDOW, KV_WIDTH), F32)],
        scratch_shapes=[
            pltpu.VMEM((T, D_MODEL), BF16),
            pltpu.VMEM((T, D_MODEL), F32),
            pltpu.VMEM((T, D_MODEL), BF16),
            pltpu.VMEM((T, D_MODEL), BF16),
            pltpu.VMEM((T, Q_WIDTH), BF16),
            pltpu.VMEM((T + SWA_WINDOW, KV_WIDTH), BF16),
            pltpu.VMEM((T + SWA_WINDOW, KV_WIDTH), BF16),
            pltpu.VMEM((T, Q_WIDTH), BF16),
        ],
        compiler_params=pltpu.CompilerParams(
            dimension_semantics=("arbitrary", "arbitrary"), vmem_limit_bytes=VMEM_LIMIT_BYTES),
        name="mixer_prompt",
    )(w["attn_sinks"], x, rope, *shared_args)


def _mixer_sample(x, cache_k, cache_v, w):
    n_seq, seq_len, _ = x.shape
    rows = n_seq * seq_len
    rope = jnp.tile(_rope_tables(PAST_LEN + jnp.arange(seq_len, dtype=F32)), (n_seq, 1))
    shared_specs, shared_args = _shared_mixer_operands(w)
    cache_rows = n_seq * SWA_WINDOW
    return pl.pallas_call(
        functools.partial(_mixer_sample_kernel, n_seq=n_seq, seq_len=seq_len),
        grid=(1,),
        in_specs=[pl.BlockSpec(memory_space=pltpu.SMEM), _resident((rows, D_MODEL)),
                  _resident((rows, N_ROPE_TABLES * LANES))]
                 + shared_specs + [_resident((cache_rows, KV_WIDTH)), _resident((cache_rows, KV_WIDTH))],
        out_specs=[_resident((rows, D_MODEL)), _resident((rows, KV_WIDTH)), _resident((rows, KV_WIDTH)),
                   _resident((rows, D_MODEL))],
        out_shape=[jax.ShapeDtypeStruct((rows, D_MODEL), F32),
                   jax.ShapeDtypeStruct((rows, KV_WIDTH), F32),
                   jax.ShapeDtypeStruct((rows, KV_WIDTH), F32),
                   jax.ShapeDtypeStruct((rows, D_MODEL), F32)],
        scratch_shapes=[
            pltpu.VMEM((rows, D_MODEL), BF16),
            pltpu.VMEM((rows, D_MODEL), BF16),
            pltpu.VMEM((rows, D_MODEL), BF16),
            pltpu.VMEM((rows, Q_WIDTH), BF16),
            pltpu.VMEM((rows, KV_WIDTH), BF16),
            pltpu.VMEM((rows, KV_WIDTH), BF16),
            pltpu.VMEM((rows, Q_WIDTH), BF16),
        ],
        compiler_params=pltpu.CompilerParams(
            dimension_semantics=("arbitrary",), vmem_limit_bytes=VMEM_LIMIT_BYTES),
        name="mixer_sample",
    )(w["attn_sinks"], x.reshape(rows, D_MODEL), rope, *shared_args,
      cache_k.reshape(cache_rows, KV_WIDTH), cache_v.reshape(cache_rows, KV_WIDTH))


def _ffn(h, w, tile, name):
    rows = h.shape[0]
    return pl.pallas_call(
        _ffn_kernel,
        grid=(rows // tile,),
        in_specs=[pl.BlockSpec((tile, D_MODEL), lambda i: (i, 0)),
                  _resident((1, D_MODEL)), _resident((1, D_MODEL)),
                  _resident((D_MODEL, D_FF)), _resident((D_FF, D_MODEL))],
        out_specs=pl.BlockSpec((tile, D_MODEL), lambda i: (i, 0)),
        out_shape=jax.ShapeDtypeStruct((rows, D_MODEL), F32),
        scratch_shapes=[pltpu.VMEM((tile, D_MODEL), BF16)],
        compiler_params=pltpu.CompilerParams(
            dimension_semantics=("arbitrary",), vmem_limit_bytes=VMEM_LIMIT_BYTES),
        name=name,
    )(h, w["g_ffn_pre"], w["g_ffn_post"], w["w_ff1"], w["w_ff2"])


def _layer(h_p, h_s, ck, cv, w):
    B, S, _ = h_p.shape
    n_seq, seq_len, _ = h_s.shape
    h_p, k_last, v_last = _mixer_prompt(h_p, w)
    y_p = _ffn(h_p.reshape(B * S, D_MODEL), w, FFN_TILE, "ffn_prompt").reshape(B, S, D_MODEL)
    h_s, k_new, v_new, v_sgu = _mixer_sample(h_s, ck, cv, w)
    y_s = _ffn(h_s, w, n_seq * seq_len, "ffn_sample").reshape(n_seq, seq_len, D_MODEL)
    return (y_p, y_s,
            k_last.reshape(B, SWA_WINDOW, KV_HEADS, HEAD_DIM), v_last.reshape(B, SWA_WINDOW, KV_HEADS, HEAD_DIM),
            k_new.reshape(n_seq, seq_len, KV_HEADS, HEAD_DIM), v_new.reshape(n_seq, seq_len, KV_HEADS, HEAD_DIM),
            v_sgu.reshape(n_seq, seq_len, D_MODEL))


def kernel(x_prompt, x_sample, cache_swa_k, cache_swa_v, w_in, sgu_ln_g, sgu_ln_b, sgu_w, sgu_b, attn_sinks,
           w_branch_a, w_branch_b, w_out, g_mix_pre, g_mix_post, g_ffn_pre, g_ffn_post, w_ff1, w_ff2):
    depth = w_in.shape[0]
    h_p, h_s = x_prompt, x_sample
    per_layer = []
    for l in range(depth):
        row = lambda v: v[l].reshape(1, -1)
        w_in_l = jnp.concatenate([w_in[l][:, :OFF_Q], _q_heads_by_group(w_in[l][:, OFF_Q:OFF_K], 1),
                                  w_in[l][:, OFF_K:]], axis=1).astype(BF16)
        w = {
            "w_in": w_in_l, "w_branch_a": w_branch_a[l].astype(BF16),
            "w_branch_b": _q_heads_by_group(w_branch_b[l], 0).astype(BF16), "w_out": w_out[l].astype(BF16),
            "w_ff1": w_ff1[l].astype(BF16), "w_ff2": w_ff2[l].astype(BF16),
            "sgu_ln_g": row(sgu_ln_g), "sgu_ln_b": row(sgu_ln_b), "sgu_w": sgu_w[l],
            "sgu_b_rows": jnp.repeat(sgu_b[l].T, SGU_GROUP_DIM, axis=1),
            "attn_sinks": attn_sinks[l],
            "g_mix_pre": row(g_mix_pre), "g_mix_post": row(g_mix_post),
            "g_ffn_pre": row(g_ffn_pre), "g_ffn_post": row(g_ffn_post),
        }
        h_p, h_s, *states = _layer(h_p, h_s, cache_swa_k[l], cache_swa_v[l], w)
        per_layer.append(states)
    stacked = [jnp.stack([states[i] for states in per_layer]) for i in range(5)]
    return (h_p, h_s, *stacked)
```
